```python
import jax, jax.numpy as jnp
from jax import lax
import numpy as np

D_MODEL = 1024
BATCH = 16
SEQ = 4096
DEPTH = 4

CTX_LEN = 256
GRID_W = 64
N_HEADS = 8
N_KV_HEADS = 2
HEAD_DIM = 64
ATTN_W = N_HEADS * HEAD_DIM
KV_W = N_KV_HEADS * HEAD_DIM
CONV_W = D_MODEL - ATTN_W
MIX_W = ATTN_W + CONV_W
IN_W = ATTN_W + 2 * KV_W + 2 * CONV_W
CONV_K = 31
D_FF = 2816
N_EXPERTS = 8
TOP_K = 2
Q_BLOCK = 128
ROPE_THETA = 10000.0
EPS = 1e-6

kernel_name = "hybrid_attn_conformer_moe_prefix_dit"


def rmsnorm(x, g):
    xf = x.astype(jnp.float32)
    y = xf * lax.rsqrt(jnp.mean(xf * xf, axis=-1, keepdims=True) + EPS)
    return (y * g.astype(jnp.float32)).astype(x.dtype)


def layernorm(x, g, b):
    xf = x.astype(jnp.float32)
    mu = jnp.mean(xf, axis=-1, keepdims=True)
    var = jnp.mean(jnp.square(xf - mu), axis=-1, keepdims=True)
    y = (xf - mu) * lax.rsqrt(var + EPS)
    return (y * g.astype(jnp.float32) + b.astype(jnp.float32)).astype(x.dtype)


def rope_tables(n_tokens):
    rows = n_tokens // GRID_W
    row = jnp.repeat(jnp.arange(rows), GRID_W).astype(jnp.float32)
    col = jnp.tile(jnp.arange(GRID_W), rows).astype(jnp.float32)
    half = HEAD_DIM // 2
    inv = ROPE_THETA ** (-jnp.arange(0, half, 2, dtype=jnp.float32) / half)
    ang = jnp.concatenate([row[:, None] * inv, col[:, None] * inv], axis=-1)
    return jnp.cos(ang), jnp.sin(ang)


def apply_rope(x, cos, sin):
    xf = x.astype(jnp.float32).reshape(*x.shape[:-1], HEAD_DIM // 2, 2)
    x0, x1 = xf[..., 0], xf[..., 1]
    c = cos[None, :, None, :]
    s = sin[None, :, None, :]
    out = jnp.stack([x0 * c - x1 * s, x0 * s + x1 * c], axis=-1)
    return out.reshape(x.shape).astype(x.dtype)


def gqa(q, k, v):
    B, L = q.shape[:2]
    G = N_HEADS // N_KV_HEADS
    qg = q.reshape(B, L, N_KV_HEADS, G, HEAD_DIM)
    s = jnp.einsum('bqkgd,bskd->bkgqs', qg, k, preferred_element_type=jnp.float32) * (HEAD_DIM ** -0.5)
    p = jax.nn.softmax(s, axis=-1).astype(v.dtype)
    o = jnp.einsum('bkgqs,bskd->bqkgd', p, v)
    return o.reshape(B, L, ATTN_W)


def blocked_gqa(q, k, v):
    B, S = q.shape[:2]
    nb = S // Q_BLOCK
    qb = q.reshape(B, nb, Q_BLOCK, N_HEADS, HEAD_DIM).swapaxes(0, 1)
    o = lax.map(lambda qi: gqa(qi, k, v), qb)
    return o.swapaxes(0, 1).reshape(B, S, ATTN_W)


def conformer_conv(u, w_dw, b_dw, ln_g, ln_b):
    a, g = jnp.split(u, 2, axis=-1)
    y = a * jax.nn.sigmoid(g)
    y = lax.conv_general_dilated(
        y, w_dw[:, None, :].astype(y.dtype), window_strides=(1,),
        padding=[(CONV_K // 2, CONV_K // 2)],
        dimension_numbers=('NWC', 'WIO', 'NWC'), feature_group_count=CONV_W) + b_dw
    return jax.nn.silu(layernorm(y, ln_g, ln_b))


def split_heads(z, n):
    return z.reshape(*z.shape[:2], n, HEAD_DIM)


def mixer(h_lat, h_ctx, cos, sin, w_in, q_g, k_g, w_dw, b_dw, ln_g, ln_b, w_out, ctx_out):
    cuts = [ATTN_W, ATTN_W + KV_W, ATTN_W + 2 * KV_W]
    q, k, v, u = jnp.split(h_lat @ w_in, cuts, axis=-1)
    if ctx_out:
        qc, kc, vc, uc = jnp.split(h_ctx @ w_in, cuts, axis=-1)
    else:
        kc, vc = jnp.split(h_ctx @ w_in[:, ATTN_W:ATTN_W + 2 * KV_W], 2, axis=-1)
    q = apply_rope(rmsnorm(split_heads(q, N_HEADS), q_g), cos, sin)
    k = apply_rope(rmsnorm(split_heads(k, N_KV_HEADS), k_g), cos, sin)
    kc = rmsnorm(split_heads(kc, N_KV_HEADS), k_g)
    vc = split_heads(vc, N_KV_HEADS)
    k_all = jnp.concatenate([k, kc], axis=1)
    v_all = jnp.concatenate([split_heads(v, N_KV_HEADS), vc], axis=1)
    a_lat = blocked_gqa(q, k_all, v_all)
    c_lat = conformer_conv(u, w_dw, b_dw, ln_g, ln_b)
    o_lat = jnp.concatenate([a_lat, c_lat], axis=-1) @ w_out
    if not ctx_out:
        return o_lat, None
    qc = rmsnorm(split_heads(qc, N_HEADS), q_g)
    a_ctx = gqa(qc, kc, vc)
    c_ctx_out = conformer_conv(uc, w_dw, b_dw, ln_g, ln_b)
    o_ctx = jnp.concatenate([a_ctx, c_ctx_out], axis=-1) @ w_out
    return o_lat, o_ctx


def swiglu(h, w1, w3, w2):
    return (jax.nn.silu(h @ w1) * (h @ w3)) @ w2


def moe_swiglu(h, w_r, b_r, w1, w3, w2):
    B, L, D = h.shape
    hf = h.reshape(B * L, D)
    logits = (hf @ w_r).astype(jnp.float32) + b_r.astype(jnp.float32)
    top_v, top_i = lax.top_k(logits, TOP_K)
    gates = jax.nn.softmax(top_v, axis=-1)
    combine = jnp.sum(jax.nn.one_hot(top_i, N_EXPERTS, dtype=jnp.float32) * gates[..., None], axis=-2)
    combine = combine.astype(h.dtype)
    out = jnp.zeros_like(hf)
    for e in range(N_EXPERTS):
        out = out + combine[:, e:e + 1] * swiglu(hf, w1[e], w3[e], w2[e])
    return out.reshape(B, L, D)


def modulation(cvec, w_ada, b_ada):
    return jnp.split(jax.nn.silu(cvec) @ w_ada + b_ada, 6, axis=-1)


def setup_inputs(seed: int = 0) -> dict:
    key = jax.random.key(seed)
    ks = jax.random.split(key, 32)
    f32 = jnp.float32
    n_dense = (DEPTH + 1) // 2
    n_moe = DEPTH // 2

    def nrm(k, shape, scale):
        return jax.random.normal(k, shape, f32) * scale

    return dict(
        x=nrm(ks[0], (BATCH, SEQ, D_MODEL), 1.0),
        c=nrm(ks[1], (BATCH, D_MODEL), 1.0),
        ctx=nrm(ks[2], (BATCH, CTX_LEN, D_MODEL), 1.0),
        c_ctx=nrm(ks[3], (D_MODEL,), 1.0),
        w_ada=nrm(ks[4], (DEPTH, D_MODEL, 6 * D_MODEL), D_MODEL ** -0.5),
        b_ada=nrm(ks[5], (DEPTH, 6 * D_MODEL), 0.01),
        norm1_g=1.0 + nrm(ks[6], (DEPTH, D_MODEL), 0.01),
        w_in=nrm(ks[7], (DEPTH, D_MODEL, IN_W), D_MODEL ** -0.5),
        q_norm_g=1.0 + nrm(ks[8], (DEPTH, HEAD_DIM), 0.01),
        k_norm_g=1.0 + nrm(ks[9], (DEPTH, HEAD_DIM), 0.01),
        dw_w=nrm(ks[10], (DEPTH, CONV_K, CONV_W), CONV_K ** -0.5),
        dw_b=nrm(ks[11], (DEPTH, CONV_W), 0.01),
        conv_ln_g=1.0 + nrm(ks[12], (DEPTH, CONV_W), 0.01),
        conv_ln_b=nrm(ks[13], (DEPTH, CONV_W), 0.01),
        w_out=nrm(ks[14], (DEPTH, MIX_W, D_MODEL), MIX_W ** -0.5),
        norm2_g=1.0 + nrm(ks[15], (DEPTH, D_MODEL), 0.01),
        ffn_w1=nrm(ks[16], (n_dense, D_MODEL, D_FF), D_MODEL ** -0.5),
        ffn_w3=nrm(ks[17], (n_dense, D_MODEL, D_FF), D_MODEL ** -0.5),
        ffn_w2=nrm(ks[18], (n_dense, D_FF, D_MODEL), D_FF ** -0.5),
        router_w=nrm(ks[19], (n_moe, D_MODEL, N_EXPERTS), D_MODEL ** -0.5),
        router_b=nrm(ks[20], (n_moe, N_EXPERTS), 0.01),
        exp_w1=nrm(ks[21], (n_moe, N_EXPERTS, D_MODEL, D_FF), D_MODEL ** -0.5),
        exp_w3=nrm(ks[22], (n_moe, N_EXPERTS, D_MODEL, D_FF), D_MODEL ** -0.5),
        exp_w2=nrm(ks[23], (n_moe, N_EXPERTS, D_FF, D_MODEL), D_FF ** -0.5),
        final_g=1.0 + nrm(ks[24], (D_MODEL,), 0.01),
    )


def reference(x, c, ctx, c_ctx, w_ada, b_ada, norm1_g, w_in, q_norm_g, k_norm_g, dw_w, dw_b,
              conv_ln_g, conv_ln_b, w_out, norm2_g, ffn_w1, ffn_w3, ffn_w2, router_w, router_b,
              exp_w1, exp_w3, exp_w2, final_g):
    cos, sin = rope_tables(x.shape[1])
    xc = ctx
    for l in range(DEPTH):
        last = l == DEPTH - 1
        sh1, sc1, g1, sh2, sc2, g2 = [m[:, None, :] for m in modulation(c, w_ada[l], b_ada[l])]
        csh1, csc1, cg1, csh2, csc2, cg2 = modulation(c_ctx, w_ada[l], b_ada[l])
        h_lat = rmsnorm(x, norm1_g[l]) * (1.0 + sc1) + sh1
        h_ctx = rmsnorm(xc, norm1_g[l]) * (1.0 + csc1) + csh1
        o_lat, o_ctx = mixer(h_lat, h_ctx, cos, sin, w_in[l], q_norm_g[l], k_norm_g[l],
                             dw_w[l], dw_b[l], conv_ln_g[l], conv_ln_b[l], w_out[l], not last)
        x = x + g1 * o_lat
        h_lat = rmsnorm(x, norm2_g[l]) * (1.0 + sc2) + sh2
        if l % 2 == 0:
            i = l // 2
            f_lat = swiglu(h_lat, ffn_w1[i], ffn_w3[i], ffn_w2[i])
        else:
            i = l // 2
            f_lat = moe_swiglu(h_lat, router_w[i], router_b[i], exp_w1[i], exp_w3[i], exp_w2[i])
        if not last:
            xc = xc + cg1 * o_ctx
            h_ctx = rmsnorm(xc, norm2_g[l]) * (1.0 + csc2) + csh2
            if l % 2 == 0:
                f_ctx = swiglu(h_ctx, ffn_w1[i], ffn_w3[i], ffn_w2[i])
            else:
                f_ctx = moe_swiglu(h_ctx, router_w[i], router_b[i], exp_w1[i], exp_w3[i], exp_w2[i])
            xc = xc + cg2 * f_ctx
        x = x + g2 * f_lat
    return rmsnorm(x, final_g)
```

```python
import functools

import numpy as np
import jax
import jax.numpy as jnp
from jax import lax
from jax.experimental import pallas as pl
from jax.experimental.pallas import tpu as pltpu

F32 = jnp.float32
BF16 = jnp.bfloat16

N_HEADS = 8
N_KV_HEADS = 2
HEAD_DIM = 64
GRID_W = 64
CONV_K = 31
N_EXPERTS = 8
ROPE_THETA = 10000.0
EPS = 1e-6

LANES = 128
BF16_ROWS = 16
VMEM_LIMIT = 56 * 1024 * 1024
NEG_BIG = -1e30


def _params(*sem):
    return pltpu.CompilerParams(dimension_semantics=sem, vmem_limit_bytes=VMEM_LIMIT)


def _silu(v):
    return v / (1.0 + jnp.exp(-v))


def _mod_kernel(c_ref, w_ref, b_ref, o_ref):
    c = c_ref[...]
    o_ref[0] = jnp.dot(_silu(c), w_ref[0], preferred_element_type=F32,
                       precision=lax.Precision.HIGHEST) + b_ref[0]


def _modulation(cvec, w_ada, b_ada):
    depth, d, n = w_ada.shape
    r = cvec.shape[0]
    tn = 1536
    return pl.pallas_call(
        _mod_kernel,
        out_shape=jax.ShapeDtypeStruct((depth, r, n), F32),
        grid=(depth, n // tn),
        in_specs=[
            pl.BlockSpec((r, d), lambda l, j: (0, 0)),
            pl.BlockSpec((1, d, tn), lambda l, j: (l, 0, j)),
            pl.BlockSpec((1, 1, tn), lambda l, j: (l, 0, j)),
        ],
        out_specs=pl.BlockSpec((1, r, tn), lambda l, j: (l, 0, j)),
        compiler_params=_params("parallel", "parallel"),
        name="modulation",
    )(cvec, w_ada, b_ada.reshape(depth, 1, n))


def _norm_mod(x, ng, sc, sh):
    ms = jnp.mean(x * x, axis=-1, keepdims=True)
    return (x * lax.rsqrt(ms + EPS)) * ng * (1.0 + sc) + sh


def _inproj_kernel(*refs, use_rope, ck):
    if use_rope:
        (x_ref, sh_ref, sc_ref, ng_ref, w_ref, gain_ref, bd_ref, cos_ref, sin_ref,
         q_ref, k_ref, vt_ref, y_ref) = refs
    else:
        (x_ref, sh_ref, sc_ref, ng_ref, w_ref, gain_ref, bd_ref,
         q_ref, k_ref, vt_ref, y_ref) = refs
    aw = N_HEADS * HEAD_DIM
    kw = N_KV_HEADS * HEAD_DIM
    qkw = aw + kw
    x = x_ref[0]
    tm = x.shape[0]
    h = _norm_mod(x, ng_ref[...], sc_ref[0], sh_ref[0]).astype(BF16)
    res = jnp.dot(h, w_ref[...], preferred_element_type=F32)

    qk = res[:, :qkw]
    sq = (qk * qk).astype(BF16)
    bd = bd_ref[...]
    parts = []
    for c0 in range(0, qkw, 2 * LANES):
        w = min(2 * LANES, qkw - c0)
        parts.append(jnp.dot(sq[:, c0:c0 + w], bd[:w, :w], preferred_element_type=F32))
    ss = jnp.concatenate(parts, axis=1)
    qk = qk * lax.rsqrt(ss * (1.0 / HEAD_DIM) + EPS) * gain_ref[...]

    if use_rope:
        cos = cos_ref[...]
        sin = sin_ref[...]
        lane = lax.broadcasted_iota(jnp.int32, (tm, LANES), 1)
        first = (lane % HEAD_DIM) < (HEAD_DIM // 2)
    outs = []
    for g in range(qkw // LANES):
        blk = qk[:, g * LANES:(g + 1) * LANES]
        if use_rope:
            half = HEAD_DIM // 2
            swapped = jnp.where(first, pltpu.roll(blk, LANES - half, 1), pltpu.roll(blk, half, 1))
            blk = blk * cos + swapped * sin
        outs.append(blk)
    q_ref[0] = jnp.concatenate(outs[:aw // LANES], axis=1).astype(BF16)
    k_ref[0] = outs[aw // LANES].astype(BF16)

    v = res[:, qkw:qkw + kw]
    for c in range(tm // ck):
        vt_ref[0, c] = v[c * ck:(c + 1) * ck, :].T.astype(BF16)

    cw = (res.shape[1] - qkw - kw) // 2
    a = res[:, qkw + kw:qkw + kw + cw]
    gt = res[:, qkw + kw + cw:]
    y_ref[0] = (a / (1.0 + jnp.exp(-gt))).astype(BF16)


def _inproj(x, sh, sc, ng, w, gain, bd, cos, sin, *, tm, ck):
    b, n, d = x.shape
    in_w = w.shape[1]
    aw = N_HEADS * HEAD_DIM
    kw = N_KV_HEADS * HEAD_DIM
    cw = (in_w - aw - 2 * kw) // 2
    use_rope = cos is not None
    bm = sh.shape[0]
    mod_map = (lambda bi, i: (bi, 0, 0)) if bm > 1 else (lambda bi, i: (0, 0, 0))
    const2 = lambda bi, i: (0, 0)
    in_specs = [
        pl.BlockSpec((1, tm, d), lambda bi, i: (bi, i, 0)),
        pl.BlockSpec((1, 1, d), mod_map),
        pl.BlockSpec((1, 1, d), mod_map),
        pl.BlockSpec((1, d), const2),
        pl.BlockSpec((d, in_w), const2),
        pl.BlockSpec((1, aw + kw), const2),
        pl.BlockSpec((2 * LANES, 2 * LANES), const2),
    ]
    args = [x, sh, sc, ng, w, gain, bd]
    if use_rope:
        in_specs += [pl.BlockSpec((tm, LANES), lambda bi, i: (i, 0))] * 2
        args += [cos, sin]
    out_shape = (
        jax.ShapeDtypeStruct((b, n, aw), BF16),
        jax.ShapeDtypeStruct((b, n, kw), BF16),
        jax.ShapeDtypeStruct((b, n // ck, kw, ck), BF16),
        jax.ShapeDtypeStruct((b, n, cw), BF16),
    )
    out_specs = (
        pl.BlockSpec((1, tm, aw), lambda bi, i: (bi, i, 0)),
        pl.BlockSpec((1, tm, kw), lambda bi, i: (bi, i, 0)),
        pl.BlockSpec((1, tm // ck, kw, ck), lambda bi, i: (bi, i, 0, 0)),
        pl.BlockSpec((1, tm, cw), lambda bi, i: (bi, i, 0)),
    )
    return pl.pallas_call(
        functools.partial(_inproj_kernel, use_rope=use_rope, ck=ck),
        out_shape=out_shape,
        grid=(b, n // tm),
        in_specs=in_specs,
        out_specs=out_specs,
        compiler_params=_params("parallel", "parallel"),
        name="inproj_lat" if use_rope else "inproj_ctx",
    )(*args)


def _attn_kernel(q_ref, *rest, n_chunks):
    nsrc = len(n_chunks)
    kv_refs = rest[:2 * nsrc]
    o_ref = rest[2 * nsrc]
    m_ref, acc_ref = rest[2 * nsrc + 1:]
    qblk = q_ref[0]
    lane = lax.broadcasted_iota(jnp.int32, qblk.shape, 1)
    for hh in range(N_KV_HEADS):
        lo = hh * HEAD_DIM
        qm = jnp.where((lane >= lo) & (lane < lo + HEAD_DIM), qblk, jnp.zeros_like(qblk))
        m_ref[...] = jnp.full(m_ref.shape, NEG_BIG, F32)
        acc_ref[...] = jnp.zeros(acc_ref.shape, F32)
        for si in range(nsrc):
            k_ref, vt_ref = kv_refs[2 * si], kv_refs[2 * si + 1]
            ck = k_ref.shape[2]

            def step(c, carry, k_ref=k_ref, vt_ref=vt_ref, ck=ck):
                kc = k_ref[0, c]
                s = lax.dot_general(kc, qm, (((1,), (1,)), ((), ())),
                                    preferred_element_type=F32)
                m_old = m_ref[...]
                m_new = jnp.maximum(m_old, jnp.max(s, axis=0, keepdims=True))
                alpha = jnp.exp(m_old - m_new)
                p = jnp.exp(s - m_new).astype(BF16)
                vc = vt_ref[0, c, lo:lo + HEAD_DIM, :]
                vaug = jnp.concatenate([vc, jnp.ones((BF16_ROWS, ck), BF16)], axis=0)
                acc_ref[...] = acc_ref[...] * alpha + jnp.dot(vaug, p, preferred_element_type=F32)
                m_ref[...] = m_new
                return carry

            if n_chunks[si] == 1:
                step(0, 0)
            else:
                lax.fori_loop(0, n_chunks[si], step, 0)
        acc = acc_ref[...]
        o = acc[:HEAD_DIM] / acc[HEAD_DIM:HEAD_DIM + 1]
        o_ref[0, lo:lo + HEAD_DIM, :] = o.astype(BF16)


def _attention(q, kvs, *, tq):
    b, n, aw = q.shape
    groups = aw // LANES
    in_specs = [pl.BlockSpec((1, tq, LANES), lambda bi, j, qi: (bi, qi, j))]
    args = [q]
    n_chunks = []
    for k, vt in kvs:
        _, nc, ck, kw = k.shape
        in_specs.append(pl.BlockSpec((1, nc, ck, kw), lambda bi, j, qi: (bi, 0, 0, 0)))
        in_specs.append(pl.BlockSpec((1, nc, kw, ck), lambda bi, j, qi: (bi, 0, 0, 0)))
        args += [k, vt]
        n_chunks.append(nc)
    return pl.pallas_call(
        functools.partial(_attn_kernel, n_chunks=tuple(n_chunks)),
        out_shape=jax.ShapeDtypeStruct((b, aw, n), BF16),
        grid=(b, groups, n // tq),
        in_specs=in_specs,
        out_specs=pl.BlockSpec((1, LANES, tq), lambda bi, j, qi: (bi, j, qi)),
        scratch_shapes=[pltpu.VMEM((1, tq), F32), pltpu.VMEM((HEAD_DIM + BF16_ROWS, tq), F32)],
        compiler_params=_params("parallel", "parallel", "parallel"),
        name="attention_lat" if len(kvs) > 1 else "attention_ctx",
    )(*args)


CONV_HALO = 16
CONV_ROWS = 32


def _conv_kernel(y_ref, yp_ref, yn_ref, w_ref, b_ref, g_ref, be_ref, o_ref, buf_ref):
    i = pl.program_id(1)
    last = pl.num_programs(1) - 1
    tm = y_ref.shape[1]
    prev = yp_ref[0].astype(F32)
    nxt = yn_ref[0].astype(F32)
    buf_ref[0:CONV_HALO, :] = jnp.where(i > 0, prev, jnp.zeros_like(prev))
    buf_ref[CONV_HALO:CONV_HALO + tm, :] = y_ref[0].astype(F32)
    buf_ref[CONV_HALO + tm:, :] = jnp.where(i < last, nxt, jnp.zeros_like(nxt))
    w = w_ref[...]
    off = CONV_HALO - CONV_K // 2
    for r in range(tm // CONV_ROWS):
        r0 = r * CONV_ROWS
        acc = jnp.zeros((CONV_ROWS, w.shape[1]), F32)
        for k in range(CONV_K):
            acc = acc + w[k:k + 1, :] * buf_ref[r0 + k + off:r0 + k + off + CONV_ROWS, :]
        acc = acc + b_ref[...]
        mu = jnp.mean(acc, axis=-1, keepdims=True)
        cen = acc - mu
        var = jnp.mean(cen * cen, axis=-1, keepdims=True)
        z = cen * lax.rsqrt(var + EPS) * g_ref[...] + be_ref[...]
        o_ref[0, r0:r0 + CONV_ROWS, :] = _silu(z).astype(BF16)


def _conformer_conv(y, w_dw, b_dw, ln_g, ln_b, *, tm):
    b, n, cw = y.shape
    hb = tm // CONV_HALO
    nh = n // CONV_HALO
    vec = pl.BlockSpec((1, cw), lambda bi, i: (0, 0))
    return pl.pallas_call(
        _conv_kernel,
        out_shape=jax.ShapeDtypeStruct((b, n, cw), BF16),
        grid=(b, n // tm),
        in_specs=[
            pl.BlockSpec((1, tm, cw), lambda bi, i: (bi, i, 0)),
            pl.BlockSpec((1, CONV_HALO, cw), lambda bi, i: (bi, jnp.maximum(i * hb - 1, 0), 0)),
            pl.BlockSpec((1, CONV_HALO, cw), lambda bi, i: (bi, jnp.minimum((i + 1) * hb, nh - 1), 0)),
            pl.BlockSpec((CONV_K, cw), lambda bi, i: (0, 0)),
            vec, vec, vec,
        ],
        out_specs=pl.BlockSpec((1, tm, cw), lambda bi, i: (bi, i, 0)),
        scratch_shapes=[pltpu.VMEM((tm + 2 * CONV_HALO, cw), F32)],
        compiler_params=_params("parallel", "parallel"),
        name="conformer_conv",
    )(y, y, y, w_dw, b_dw.reshape(1, cw), ln_g.reshape(1, cw), ln_b.reshape(1, cw))


def _outproj_kernel(at_ref, c_ref, wa_ref, wc_ref, x_ref, g_ref, o_ref):
    o = lax.dot_general(at_ref[0], wa_ref[...], (((0,), (0,)), ((), ())),
                        preferred_element_type=F32)
    o = o + jnp.dot(c_ref[0], wc_ref[...], preferred_element_type=F32)
    o_ref[0] = x_ref[0] + g_ref[0] * o


def _outproj(a_t, c, w_a, w_c, x, gate, *, tm):
    b, n, d = x.shape
    aw = a_t.shape[1]
    cw = c.shape[2]
    bm = gate.shape[0]
    mod_map = (lambda bi, i: (bi, 0, 0)) if bm > 1 else (lambda bi, i: (0, 0, 0))
    return pl.pallas_call(
        _outproj_kernel,
        out_shape=jax.ShapeDtypeStruct((b, n, d), F32),
        grid=(b, n // tm),
        in_specs=[
            pl.BlockSpec((1, aw, tm), lambda bi, i: (bi, 0, i)),
            pl.BlockSpec((1, tm, cw), lambda bi, i: (bi, i, 0)),
            pl.BlockSpec((aw, d), lambda bi, i: (0, 0)),
            pl.BlockSpec((cw, d), lambda bi, i: (0, 0)),
            pl.BlockSpec((1, tm, d), lambda bi, i: (bi, i, 0)),
            pl.BlockSpec((1, 1, d), mod_map),
        ],
        out_specs=pl.BlockSpec((1, tm, d), lambda bi, i: (bi, i, 0)),
        compiler_params=_params("parallel", "parallel"),
        name="outproj",
    )(a_t, c, w_a, w_c, x, gate)


def _ffn_kernel(*refs, routed, ff_chunk):
    if routed:
        (x_ref, sh_ref, sc_ref, g_ref, ng_ref, w1_ref, w3_ref, w2_ref, wr_ref, br_ref,
         o_ref, h_ref, comb_ref, acc_ref) = refs
    else:
        (x_ref, sh_ref, sc_ref, g_ref, ng_ref, w1_ref, w3_ref, w2_ref, o_ref) = refs
    e = pl.program_id(2)
    n_e = pl.num_programs(2)

    def hidden():
        return _norm_mod(x_ref[0], ng_ref[...], sc_ref[0], sh_ref[0])

    def swiglu(h):
        d_ff = w1_ref.shape[2]
        f = None
        for c0 in range(0, d_ff, ff_chunk):
            a = jnp.dot(h, w1_ref[0, :, c0:c0 + ff_chunk], preferred_element_type=F32)
            b = jnp.dot(h, w3_ref[0, :, c0:c0 + ff_chunk], preferred_element_type=F32)
            z = (_silu(a) * b).astype(BF16)
            part = jnp.dot(z, w2_ref[0, c0:c0 + ff_chunk, :], preferred_element_type=F32)
            f = part if f is None else f + part
        return f

    if not routed:
        f = swiglu(hidden().astype(BF16))
        o_ref[0] = x_ref[0] + g_ref[0] * f
        return

    @pl.when(e == 0)
    def _():
        h = hidden()
        h_ref[...] = h.astype(BF16)
        logits = jnp.dot(h, wr_ref[...], preferred_element_type=F32,
                         precision=lax.Precision.HIGHEST) + br_ref[...]
        lane = lax.broadcasted_iota(jnp.int32, logits.shape, 1)
        t1 = jnp.max(logits, axis=-1, keepdims=True)
        i1 = jnp.min(jnp.where(logits == t1, lane, LANES), axis=-1, keepdims=True)
        rest = jnp.where(lane == i1, NEG_BIG, logits)
        t2 = jnp.max(rest, axis=-1, keepdims=True)
        i2 = jnp.min(jnp.where(rest == t2, lane, LANES), axis=-1, keepdims=True)
        e2 = jnp.exp(t2 - t1)
        den = 1.0 + e2
        comb_ref[...] = jnp.where(lane == i1, 1.0 / den, 0.0) + jnp.where(lane == i2, e2 / den, 0.0)
        acc_ref[...] = jnp.zeros(acc_ref.shape, F32)

    f = swiglu(h_ref[...])
    lane = lax.broadcasted_iota(jnp.int32, comb_ref.shape, 1)
    gate_e = jnp.sum(jnp.where(lane == e, comb_ref[...], 0.0), axis=-1, keepdims=True)
    acc_ref[...] = acc_ref[...] + gate_e * f

    @pl.when(e == n_e - 1)
    def _():
        o_ref[0] = x_ref[0] + g_ref[0] * acc_ref[...]


def _ffn(x, sh, sc, gate, ng, w1, w3, w2, router=None, *, tm):
    b, n, d = x.shape
    n_e, _, d_ff = w1.shape
    routed = router is not None
    bm = sh.shape[0]
    mod_map = (lambda bi, i, e: (bi, 0, 0)) if bm > 1 else (lambda bi, i, e: (0, 0, 0))
    mod = pl.BlockSpec((1, 1, d), mod_map)
    in_specs = [
        pl.BlockSpec((1, tm, d), lambda bi, i, e: (bi, i, 0)),
        mod, mod, mod,
        pl.BlockSpec((1, d), lambda bi, i, e: (0, 0)),
        pl.BlockSpec((1, d, d_ff), lambda bi, i, e: (e, 0, 0)),
        pl.BlockSpec((1, d, d_ff), lambda bi, i, e: (e, 0, 0)),
        pl.BlockSpec((1, d_ff, d), lambda bi, i, e: (e, 0, 0)),
    ]
    args = [x, sh, sc, gate, ng, w1, w3, w2]
    scratch = []
    if routed:
        in_specs += [pl.BlockSpec((d, LANES), lambda bi, i, e: (0, 0)),
                     pl.BlockSpec((1, LANES), lambda bi, i, e: (0, 0))]
        args += list(router)
        scratch = [pltpu.VMEM((tm, d), BF16), pltpu.VMEM((tm, LANES), F32), pltpu.VMEM((tm, d), F32)]
    return pl.pallas_call(
        functools.partial(_ffn_kernel, routed=routed, ff_chunk=d_ff // 2),
        out_shape=jax.ShapeDtypeStruct((b, n, d), F32),
        grid=(b, n // tm, n_e),
        in_specs=in_specs,
        out_specs=pl.BlockSpec((1, tm, d), lambda bi, i, e: (bi, i, 0)),
        scratch_shapes=scratch,
        compiler_params=_params("parallel", "parallel", "arbitrary"),
        name="moe_ffn" if routed else "dense_ffn",
    )(*args)


def _final_kernel(x_ref, g_ref, o_ref):
    x = x_ref[0]
    ms = jnp.mean(x * x, axis=-1, keepdims=True)
    o_ref[0] = x * lax.rsqrt(ms + EPS) * g_ref[...]


def _final_norm(x, g, *, tm):
    b, n, d = x.shape
    return pl.pallas_call(
        _final_kernel,
        out_shape=jax.ShapeDtypeStruct((b, n, d), F32),
        grid=(b, n // tm),
        in_specs=[pl.BlockSpec((1, tm, d), lambda bi, i: (bi, i, 0)),
                  pl.BlockSpec((1, d), lambda bi, i: (0, 0))],
        out_specs=pl.BlockSpec((1, tm, d), lambda bi, i: (bi, i, 0)),
        compiler_params=_params("parallel", "parallel"),
        name="final_norm",
    )(x, g.reshape(1, d))


def _qk_column_order():
    half = np.concatenate([np.arange(0, HEAD_DIM, 2), np.arange(1, HEAD_DIM, 2)])
    per_group = N_HEADS // N_KV_HEADS
    heads = [h for j in range(per_group) for h in range(j, N_HEADS, per_group)]
    q_cols = np.concatenate([h * HEAD_DIM + half for h in heads])
    k_cols = N_HEADS * HEAD_DIM + np.concatenate([g * HEAD_DIM + half for g in range(N_KV_HEADS)])
    head_rows = np.concatenate([h * HEAD_DIM + np.arange(HEAD_DIM) for h in heads])
    return half, q_cols, k_cols, head_rows


def _rope_tables(n_tokens):
    rows = n_tokens // GRID_W
    row = jnp.repeat(jnp.arange(rows), GRID_W).astype(F32)
    col = jnp.tile(jnp.arange(GRID_W), rows).astype(F32)
    half = HEAD_DIM // 2
    inv = ROPE_THETA ** (-jnp.arange(0, half, 2, dtype=F32) / half)
    ang = jnp.concatenate([row[:, None] * inv, col[:, None] * inv], axis=-1)
    cos, sin = jnp.cos(ang), jnp.sin(ang)
    reps = LANES // HEAD_DIM
    return (jnp.tile(jnp.concatenate([cos, cos], axis=-1), (1, reps)),
            jnp.tile(jnp.concatenate([-sin, sin], axis=-1), (1, reps)))


def _pick(n, pref):
    t = min(n, pref)
    while n % t:
        t //= 2
    return t


def kernel(x, c, ctx, c_ctx, w_ada, b_ada, norm1_g, w_in, q_norm_g, k_norm_g, dw_w, dw_b,
           conv_ln_g, conv_ln_b, w_out, norm2_g, ffn_w1, ffn_w3, ffn_w2, router_w, router_b,
           exp_w1, exp_w3, exp_w2, final_g):
    bsz, s, d = x.shape
    n_ctx = ctx.shape[1]
    depth = w_ada.shape[0]
    aw = N_HEADS * HEAD_DIM
    kw = N_KV_HEADS * HEAD_DIM

    half, q_cols, k_cols, head_rows = _qk_column_order()
    cols = np.concatenate([q_cols, k_cols, np.arange(aw + kw, w_in.shape[2])])
    w_in_p = w_in[:, :, cols].astype(BF16)
    gain = jnp.concatenate([jnp.tile(q_norm_g[:, half], (1, N_HEADS)) * (HEAD_DIM ** -0.5),
                            jnp.tile(k_norm_g[:, half], (1, N_KV_HEADS))], axis=1)
    w_out_a = w_out[:, head_rows, :].astype(BF16)
    w_out_c = w_out[:, aw:, :].astype(BF16)
    blk = np.kron(np.eye(2 * LANES // HEAD_DIM), np.ones((HEAD_DIM, HEAD_DIM)))
    bd = jnp.asarray(blk, BF16)
    cos, sin = _rope_tables(s)

    pad = (-(bsz + 1)) % 8
    cvec = jnp.concatenate([c, c_ctx[None, :], jnp.zeros((pad, d), F32)], axis=0)
    mods = _modulation(cvec, w_ada, b_ada)

    tm_lat = _pick(s, 1024)
    ck_lat = _pick(s, 512)
    tq_lat = _pick(s, 512)
    tm_ctx = n_ctx
    tm_conv = _pick(s, 256)
    tm_ffn = _pick(s, 512)

    xc = ctx
    for l in range(depth):
        last = l == depth - 1
        m = [mods[l, :, i * d:(i + 1) * d] for i in range(6)]
        lat = [v[:bsz, None, :] for v in m]
        cx = [v[bsz:bsz + 1, None, :] for v in m]
        ng1 = norm1_g[l].reshape(1, d)
        ng2 = norm2_g[l].reshape(1, d)
        gl = gain[l].reshape(1, aw + kw)

        q, k, vt, y = _inproj(x, lat[0], lat[1], ng1, w_in_p[l], gl, bd, cos, sin, tm=tm_lat, ck=ck_lat)
        qc, kc, vtc, yc = _inproj(xc, cx[0], cx[1], ng1, w_in_p[l], gl, bd, None, None,
                                  tm=tm_ctx, ck=n_ctx)
        k4 = k.reshape(bsz, s // ck_lat, ck_lat, kw)
        kc4 = kc.reshape(bsz, 1, n_ctx, kw)
        a_t = _attention(q, [(k4, vt), (kc4, vtc)], tq=tq_lat)
        c_lat = _conformer_conv(y, dw_w[l], dw_b[l], conv_ln_g[l], conv_ln_b[l], tm=tm_conv)
        x = _outproj(a_t, c_lat, w_out_a[l], w_out_c[l], x, lat[2], tm=tm_ffn)
        if not last:
            a_tc = _attention(qc, [(kc4, vtc)], tq=n_ctx)
            c_c = _conformer_conv(yc, dw_w[l], dw_b[l], conv_ln_g[l], conv_ln_b[l], tm=n_ctx)
            xc = _outproj(a_tc, c_c, w_out_a[l], w_out_c[l], xc, cx[2], tm=n_ctx)

        i = l // 2
        if l % 2 == 0:
            w1 = ffn_w1[i:i + 1].astype(BF16)
            w3 = ffn_w3[i:i + 1].astype(BF16)
            w2 = ffn_w2[i:i + 1].astype(BF16)
            router = None
        else:
            w1 = exp_w1[i].astype(BF16)
            w3 = exp_w3[i].astype(BF16)
            w2 = exp_w2[i].astype(BF16)
            w_r = jnp.zeros((d, LANES), F32).at[:, :N_EXPERTS].set(router_w[i])
            b_r = jnp.full((1, LANES), NEG_BIG, F32).at[0, :N_EXPERTS].set(router_b[i])
            router = (w_r, b_r)
        x = _ffn(x, lat[3], lat[4], lat[5], ng2, w1, w3, w2, router, tm=tm_ffn)
        if not last:
            xc = _ffn(xc, cx[3], cx[4], cx[5], ng2, w1, w3, w2, router, tm=n_ctx)
    return _final_norm(x, final_g, tm=tm_ffn)
```

```python
import functools

import numpy as np
import jax
import jax.numpy as jnp
from jax import lax
from jax.experimental import pallas as pl
from jax.experimental.pallas import tpu as pltpu

F32 = jnp.float32
BF16 = jnp.bfloat16

N_HEADS = 8
N_KV_HEADS = 2
HEAD_DIM = 64
GRID_W = 64
CONV_K = 31
N_EXPERTS = 8
ROPE_THETA = 10000.0
EPS = 1e-6

LANES = 128
BF16_ROWS = 16
VMEM_LIMIT = 56 * 1024 * 1024
NEG_BIG = -1e30


def _params(*sem):
    return pltpu.CompilerParams(dimension_semantics=sem, vmem_limit_bytes=VMEM_LIMIT)


def _silu(v):
    return v / (1.0 + jnp.exp(-v))


def _mod_kernel(c_ref, w_ref, b_ref, o_ref):
    c = c_ref[...]
    o_ref[0] = jnp.dot(_silu(c), w_ref[0], preferred_element_type=F32,
                       precision=lax.Precision.HIGHEST) + b_ref[0]


def _modulation(cvec, w_ada, b_ada):
    depth, d, n = w_ada.shape
    r = cvec.shape[0]
    tn = 1536
    return pl.pallas_call(
        _mod_kernel,
        out_shape=jax.ShapeDtypeStruct((depth, r, n), F32),
        grid=(depth, n // tn),
        in_specs=[
            pl.BlockSpec((r, d), lambda l, j: (0, 0)),
            pl.BlockSpec((1, d, tn), lambda l, j: (l, 0, j)),
            pl.BlockSpec((1, 1, tn), lambda l, j: (l, 0, j)),
        ],
        out_specs=pl.BlockSpec((1, r, tn), lambda l, j: (l, 0, j)),
        compiler_params=_params("parallel", "parallel"),
        name="modulation",
    )(cvec, w_ada, b_ada.reshape(depth, 1, n))


def _norm_mod(x, ng, sc, sh):
    ms = jnp.mean(x * x, axis=-1, keepdims=True)
    return (x * lax.rsqrt(ms + EPS)) * ng * (1.0 + sc) + sh


def _inproj_kernel(*refs, use_rope, ck):
    if use_rope:
        (x_ref, sh_ref, sc_ref, ng_ref, w_ref, gain_ref, bd_ref, cos_ref, sin_ref,
         q_ref, k_ref, vt_ref, y_ref) = refs
    else:
        (x_ref, sh_ref, sc_ref, ng_ref, w_ref, gain_ref, bd_ref,
         q_ref, k_ref, vt_ref, y_ref) = refs
    aw = N_HEADS * HEAD_DIM
    kw = N_KV_HEADS * HEAD_DIM
    qkw = aw + kw
    x = x_ref[0]
    tm = x.shape[0]
    h = _norm_mod(x, ng_ref[...], sc_ref[0], sh_ref[0]).astype(BF16)
    res = jnp.dot(h, w_ref[...], preferred_element_type=F32)

    qk = res[:, :qkw]
    sq = (qk * qk).astype(BF16)
    bd = bd_ref[...]
    parts = []
    for c0 in range(0, qkw, 2 * LANES):
        w = min(2 * LANES, qkw - c0)
        parts.append(jnp.dot(sq[:, c0:c0 + w], bd[:w, :w], preferred_element_type=F32))
    ss = jnp.concatenate(parts, axis=1)
    qk = qk * lax.rsqrt(ss * (1.0 / HEAD_DIM) + EPS) * gain_ref[...]

    if use_rope:
        cos = cos_ref[...]
        sin = sin_ref[...]
        lane = lax.broadcasted_iota(jnp.int32, (tm, LANES), 1)
        first = (lane % HEAD_DIM) < (HEAD_DIM // 2)
    outs = []
    for g in range(qkw // LANES):
        blk = qk[:, g * LANES:(g + 1) * LANES]
        if use_rope:
            half = HEAD_DIM // 2
            swapped = jnp.where(first, pltpu.roll(blk, LANES - half, 1), pltpu.roll(blk, half, 1))
            blk = blk * cos + swapped * sin
        outs.append(blk)
    q_ref[0] = jnp.concatenate(outs[:aw // LANES], axis=1).astype(BF16)
    k_ref[0] = outs[aw // LANES].astype(BF16)

    v = res[:, qkw:qkw + kw]
    for c in range(tm // ck):
        vt_ref[0, c] = v[c * ck:(c + 1) * ck, :].T.astype(BF16)

    cw = (res.shape[1] - qkw - kw) // 2
    a = res[:, qkw + kw:qkw + kw + cw]
    gt = res[:, qkw + kw + cw:]
    y_ref[0] = (a / (1.0 + jnp.exp(-gt))).astype(BF16)


def _inproj(x, sh, sc, ng, w, gain, bd, cos, sin, *, tm, ck):
    b, n, d = x.shape
    in_w = w.shape[1]
    aw = N_HEADS * HEAD_DIM
    kw = N_KV_HEADS * HEAD_DIM
    cw = (in_w - aw - 2 * kw) // 2
    use_rope = cos is not None
    bm = sh.shape[0]
    mod_map = (lambda bi, i: (bi, 0, 0)) if bm > 1 else (lambda bi, i: (0, 0, 0))
    const2 = lambda bi, i: (0, 0)
    in_specs = [
        pl.BlockSpec((1, tm, d), lambda bi, i: (bi, i, 0)),
        pl.BlockSpec((1, 1, d), mod_map),
        pl.BlockSpec((1, 1, d), mod_map),
        pl.BlockSpec((1, d), const2),
        pl.BlockSpec((d, in_w), const2),
        pl.BlockSpec((1, aw + kw), const2),
        pl.BlockSpec((2 * LANES, 2 * LANES), const2),
    ]
    args = [x, sh, sc, ng, w, gain, bd]
    if use_rope:
        in_specs += [pl.BlockSpec((tm, LANES), lambda bi, i: (i, 0))] * 2
        args += [cos, sin]
    out_shape = (
        jax.ShapeDtypeStruct((b, n, aw), BF16),
        jax.ShapeDtypeStruct((b, n, kw), BF16),
        jax.ShapeDtypeStruct((b, n // ck, kw, ck), BF16),
        jax.ShapeDtypeStruct((b, n, cw), BF16),
    )
    out_specs = (
        pl.BlockSpec((1, tm, aw), lambda bi, i: (bi, i, 0)),
        pl.BlockSpec((1, tm, kw), lambda bi, i: (bi, i, 0)),
        pl.BlockSpec((1, tm // ck, kw, ck), lambda bi, i: (bi, i, 0, 0)),
        pl.BlockSpec((1, tm, cw), lambda bi, i: (bi, i, 0)),
    )
    return pl.pallas_call(
        functools.partial(_inproj_kernel, use_rope=use_rope, ck=ck),
        out_shape=out_shape,
        grid=(b, n // tm),
        in_specs=in_specs,
        out_specs=out_specs,
        compiler_params=_params("parallel", "parallel"),
        name="inproj_lat" if use_rope else "inproj_ctx",
    )(*args)


def _attn_kernel(q_ref, k_ref, vt_ref, o_ref, m_ref, acc_ref, sa_ref, sb_ref, *, n_chunks):
    qblk = q_ref[0]
    ck = k_ref.shape[2]
    lane = lax.broadcasted_iota(jnp.int32, qblk.shape, 1)
    qms = [jnp.where((lane >= hh * HEAD_DIM) & (lane < (hh + 1) * HEAD_DIM), qblk, jnp.zeros_like(qblk))
           for hh in range(N_KV_HEADS)]
    m_ref[...] = jnp.full(m_ref.shape, NEG_BIG, F32)
    acc_ref[...] = jnp.zeros(acc_ref.shape, F32)

    def scores(c, s_ref):
        kc = k_ref[0, c]
        for hh in range(N_KV_HEADS):
            s_ref[hh] = lax.dot_general(kc, qms[hh], (((1,), (1,)), ((), ())),
                                        preferred_element_type=F32)

    def consume(c, s_ref):
        for hh in range(N_KV_HEADS):
            s = s_ref[hh]
            m_old = m_ref[hh]
            m_new = jnp.maximum(m_old, jnp.max(s, axis=0, keepdims=True))
            alpha = jnp.exp2(m_old - m_new)
            p = jnp.exp2(s - m_new).astype(BF16)
            vc = vt_ref[0, c, hh * HEAD_DIM:(hh + 1) * HEAD_DIM, :]
            vaug = jnp.concatenate([vc, jnp.ones((BF16_ROWS, ck), BF16)], axis=0)
            acc_ref[hh] = acc_ref[hh] * alpha + jnp.dot(vaug, p, preferred_element_type=F32)
            m_ref[hh] = m_new

    scores(0, sa_ref)
    n_pairs = (n_chunks - 1) // 2
    if n_pairs > 0:
        def body(j, carry):
            c = 2 * j
            scores(c + 1, sb_ref)
            consume(c, sa_ref)
            scores(c + 2, sa_ref)
            consume(c + 1, sb_ref)
            return carry
        lax.fori_loop(0, n_pairs, body, 0)
    if n_chunks - 2 * n_pairs == 1:
        consume(n_chunks - 1, sa_ref)
    else:
        scores(n_chunks - 1, sb_ref)
        consume(n_chunks - 2, sa_ref)
        consume(n_chunks - 1, sb_ref)

    for hh in range(N_KV_HEADS):
        acc = acc_ref[hh]
        o = acc[:HEAD_DIM] / acc[HEAD_DIM:HEAD_DIM + 1]
        o_ref[0, hh * HEAD_DIM:(hh + 1) * HEAD_DIM, :] = o.astype(BF16)


def _attention(q, k, vt, *, tq):
    b, n, aw = q.shape
    groups = aw // LANES
    _, nc, ck, kw = k.shape
    return pl.pallas_call(
        functools.partial(_attn_kernel, n_chunks=nc),
        out_shape=jax.ShapeDtypeStruct((b, aw, n), BF16),
        grid=(b, groups, n // tq),
        in_specs=[
            pl.BlockSpec((1, tq, LANES), lambda bi, j, qi: (bi, qi, j)),
            pl.BlockSpec((1, nc, ck, kw), lambda bi, j, qi: (bi, 0, 0, 0)),
            pl.BlockSpec((1, nc, kw, ck), lambda bi, j, qi: (bi, 0, 0, 0)),
        ],
        out_specs=pl.BlockSpec((1, LANES, tq), lambda bi, j, qi: (bi, j, qi)),
        scratch_shapes=[pltpu.VMEM((N_KV_HEADS, 1, tq), F32),
                        pltpu.VMEM((N_KV_HEADS, HEAD_DIM + BF16_ROWS, tq), F32),
                        pltpu.VMEM((N_KV_HEADS, ck, tq), F32),
                        pltpu.VMEM((N_KV_HEADS, ck, tq), F32)],
        compiler_params=_params("parallel", "parallel", "parallel"),
        name="attention_lat" if nc > 1 else "attention_ctx",
    )(q, k, vt)


CONV_HALO = 16
CONV_ROWS = 32


def _conv_kernel(y_ref, yp_ref, yn_ref, w_ref, b_ref, g_ref, be_ref, o_ref, buf_ref):
    i = pl.program_id(1)
    last = pl.num_programs(1) - 1
    tm = y_ref.shape[1]
    prev = yp_ref[0].astype(F32)
    nxt = yn_ref[0].astype(F32)
    buf_ref[0:CONV_HALO, :] = jnp.where(i > 0, prev, jnp.zeros_like(prev))
    buf_ref[CONV_HALO:CONV_HALO + tm, :] = y_ref[0].astype(F32)
    buf_ref[CONV_HALO + tm:, :] = jnp.where(i < last, nxt, jnp.zeros_like(nxt))
    w = w_ref[...]
    off = CONV_HALO - CONV_K // 2
    for r in range(tm // CONV_ROWS):
        r0 = r * CONV_ROWS
        acc = jnp.zeros((CONV_ROWS, w.shape[1]), F32)
        for k in range(CONV_K):
            acc = acc + w[k:k + 1, :] * buf_ref[r0 + k + off:r0 + k + off + CONV_ROWS, :]
        acc = acc + b_ref[...]
        mu = jnp.mean(acc, axis=-1, keepdims=True)
        cen = acc - mu
        var = jnp.mean(cen * cen, axis=-1, keepdims=True)
        z = cen * lax.rsqrt(var + EPS) * g_ref[...] + be_ref[...]
        o_ref[0, r0:r0 + CONV_ROWS, :] = _silu(z).astype(BF16)


def _conformer_conv(y, w_dw, b_dw, ln_g, ln_b, *, tm):
    b, n, cw = y.shape
    hb = tm // CONV_HALO
    nh = n // CONV_HALO
    vec = pl.BlockSpec((1, cw), lambda bi, i: (0, 0))
    return pl.pallas_call(
        _conv_kernel,
        out_shape=jax.ShapeDtypeStruct((b, n, cw), BF16),
        grid=(b, n // tm),
        in_specs=[
            pl.BlockSpec((1, tm, cw), lambda bi, i: (bi, i, 0)),
            pl.BlockSpec((1, CONV_HALO, cw), lambda bi, i: (bi, jnp.maximum(i * hb - 1, 0), 0)),
            pl.BlockSpec((1, CONV_HALO, cw), lambda bi, i: (bi, jnp.minimum((i + 1) * hb, nh - 1), 0)),
            pl.BlockSpec((CONV_K, cw), lambda bi, i: (0, 0)),
            vec, vec, vec,
        ],
        out_specs=pl.BlockSpec((1, tm, cw), lambda bi, i: (bi, i, 0)),
        scratch_shapes=[pltpu.VMEM((tm + 2 * CONV_HALO, cw), F32)],
        compiler_params=_params("parallel", "parallel"),
        name="conformer_conv",
    )(y, y, y, w_dw, b_dw.reshape(1, cw), ln_g.reshape(1, cw), ln_b.reshape(1, cw))


def _outproj_kernel(at_ref, c_ref, wa_ref, wc_ref, x_ref, g_ref, o_ref):
    o = lax.dot_general(at_ref[0], wa_ref[...], (((0,), (0,)), ((), ())),
                        preferred_element_type=F32)
    o = o + jnp.dot(c_ref[0], wc_ref[...], preferred_element_type=F32)
    o_ref[0] = x_ref[0] + g_ref[0] * o


def _outproj(a_t, c, w_a, w_c, x, gate, *, tm):
    b, n, d = x.shape
    aw = a_t.shape[1]
    cw = c.shape[2]
    bm = gate.shape[0]
    mod_map = (lambda bi, i: (bi, 0, 0)) if bm > 1 else (lambda bi, i: (0, 0, 0))
    return pl.pallas_call(
        _outproj_kernel,
        out_shape=jax.ShapeDtypeStruct((b, n, d), F32),
        grid=(b, n // tm),
        in_specs=[
            pl.BlockSpec((1, aw, tm), lambda bi, i: (bi, 0, i)),
            pl.BlockSpec((1, tm, cw), lambda bi, i: (bi, i, 0)),
            pl.BlockSpec((aw, d), lambda bi, i: (0, 0)),
            pl.BlockSpec((cw, d), lambda bi, i: (0, 0)),
            pl.BlockSpec((1, tm, d), lambda bi, i: (bi, i, 0)),
            pl.BlockSpec((1, 1, d), mod_map),
        ],
        out_specs=pl.BlockSpec((1, tm, d), lambda bi, i: (bi, i, 0)),
        compiler_params=_params("parallel", "parallel"),
        name="outproj",
    )(a_t, c, w_a, w_c, x, gate)


def _ffn_kernel(*refs, routed, ff_chunk):
    if routed:
        (x_ref, sh_ref, sc_ref, g_ref, ng_ref, w1_ref, w3_ref, w2_ref, wr_ref, br_ref,
         o_ref, h_ref, comb_ref, acc_ref) = refs
    else:
        (x_ref, sh_ref, sc_ref, g_ref, ng_ref, w1_ref, w3_ref, w2_ref, o_ref) = refs
    e = pl.program_id(2)
    n_e = pl.num_programs(2)

    def hidden():
        return _norm_mod(x_ref[0], ng_ref[...], sc_ref[0], sh_ref[0])

    def swiglu(h):
        d_ff = w1_ref.shape[2]
        f = None
        for c0 in range(0, d_ff, ff_chunk):
            a = jnp.dot(h, w1_ref[0, :, c0:c0 + ff_chunk], preferred_element_type=F32)
            b = jnp.dot(h, w3_ref[0, :, c0:c0 + ff_chunk], preferred_element_type=F32)
            z = (_silu(a) * b).astype(BF16)
            part = jnp.dot(z, w2_ref[0, c0:c0 + ff_chunk, :], preferred_element_type=F32)
            f = part if f is None else f + part
        return f

    if not routed:
        f = swiglu(hidden().astype(BF16))
        o_ref[0] = x_ref[0] + g_ref[0] * f
        return

    @pl.when(e == 0)
    def _():
        h = hidden()
        h_ref[...] = h.astype(BF16)
        logits = jnp.dot(h, wr_ref[...], preferred_element_type=F32,
                         precision=lax.Precision.HIGHEST) + br_ref[...]
        lane = lax.broadcasted_iota(jnp.int32, logits.shape, 1)
        t1 = jnp.max(logits, axis=-1, keepdims=True)
        i1 = jnp.min(jnp.where(logits == t1, lane, LANES), axis=-1, keepdims=True)
        rest = jnp.where(lane == i1, NEG_BIG, logits)
        t2 = jnp.max(rest, axis=-1, keepdims=True)
        i2 = jnp.min(jnp.where(rest == t2, lane, LANES), axis=-1, keepdims=True)
        e2 = jnp.exp(t2 - t1)
        den = 1.0 + e2
        comb_ref[...] = jnp.where(lane == i1, 1.0 / den, 0.0) + jnp.where(lane == i2, e2 / den, 0.0)
        acc_ref[...] = jnp.zeros(acc_ref.shape, F32)

    f = swiglu(h_ref[...])
    lane = lax.broadcasted_iota(jnp.int32, comb_ref.shape, 1)
    gate_e = jnp.sum(jnp.where(lane == e, comb_ref[...], 0.0), axis=-1, keepdims=True)
    acc_ref[...] = acc_ref[...] + gate_e * f

    @pl.when(e == n_e - 1)
    def _():
        o_ref[0] = x_ref[0] + g_ref[0] * acc_ref[...]


def _ffn(x, sh, sc, gate, ng, w1, w3, w2, router=None, *, tm):
    b, n, d = x.shape
    n_e, _, d_ff = w1.shape
    routed = router is not None
    bm = sh.shape[0]
    mod_map = (lambda bi, i, e: (bi, 0, 0)) if bm > 1 else (lambda bi, i, e: (0, 0, 0))
    mod = pl.BlockSpec((1, 1, d), mod_map)
    in_specs = [
        pl.BlockSpec((1, tm, d), lambda bi, i, e: (bi, i, 0)),
        mod, mod, mod,
        pl.BlockSpec((1, d), lambda bi, i, e: (0, 0)),
        pl.BlockSpec((1, d, d_ff), lambda bi, i, e: (e, 0, 0)),
        pl.BlockSpec((1, d, d_ff), lambda bi, i, e: (e, 0, 0)),
        pl.BlockSpec((1, d_ff, d), lambda bi, i, e: (e, 0, 0)),
    ]
    args = [x, sh, sc, gate, ng, w1, w3, w2]
    scratch = []
    if routed:
        in_specs += [pl.BlockSpec((d, LANES), lambda bi, i, e: (0, 0)),
                     pl.BlockSpec((1, LANES), lambda bi, i, e: (0, 0))]
        args += list(router)
        scratch = [pltpu.VMEM((tm, d), BF16), pltpu.VMEM((tm, LANES), F32), pltpu.VMEM((tm, d), F32)]
    return pl.pallas_call(
        functools.partial(_ffn_kernel, routed=routed, ff_chunk=d_ff // 2),
        out_shape=jax.ShapeDtypeStruct((b, n, d), F32),
        grid=(b, n // tm, n_e),
        in_specs=in_specs,
        out_specs=pl.BlockSpec((1, tm, d), lambda bi, i, e: (bi, i, 0)),
        scratch_shapes=scratch,
        compiler_params=_params("parallel", "parallel", "arbitrary"),
        name="moe_ffn" if routed else "dense_ffn",
    )(*args)


def _final_kernel(x_ref, g_ref, o_ref):
    x = x_ref[0]
    ms = jnp.mean(x * x, axis=-1, keepdims=True)
    o_ref[0] = x * lax.rsqrt(ms + EPS) * g_ref[...]


def _final_norm(x, g, *, tm):
    b, n, d = x.shape
    return pl.pallas_call(
        _final_kernel,
        out_shape=jax.ShapeDtypeStruct((b, n, d), F32),
        grid=(b, n // tm),
        in_specs=[pl.BlockSpec((1, tm, d), lambda bi, i: (bi, i, 0)),
                  pl.BlockSpec((1, d), lambda bi, i: (0, 0))],
        out_specs=pl.BlockSpec((1, tm, d), lambda bi, i: (bi, i, 0)),
        compiler_params=_params("parallel", "parallel"),
        name="final_norm",
    )(x, g.reshape(1, d))


def _qk_column_order():
    half = np.concatenate([np.arange(0, HEAD_DIM, 2), np.arange(1, HEAD_DIM, 2)])
    per_group = N_HEADS // N_KV_HEADS
    heads = [h for j in range(per_group) for h in range(j, N_HEADS, per_group)]
    q_cols = np.concatenate([h * HEAD_DIM + half for h in heads])
    k_cols = N_HEADS * HEAD_DIM + np.concatenate([g * HEAD_DIM + half for g in range(N_KV_HEADS)])
    head_rows = np.concatenate([h * HEAD_DIM + np.arange(HEAD_DIM) for h in heads])
    return half, q_cols, k_cols, head_rows


def _rope_tables(n_tokens):
    rows = n_tokens // GRID_W
    row = jnp.repeat(jnp.arange(rows), GRID_W).astype(F32)
    col = jnp.tile(jnp.arange(GRID_W), rows).astype(F32)
    half = HEAD_DIM // 2
    inv = ROPE_THETA ** (-jnp.arange(0, half, 2, dtype=F32) / half)
    ang = jnp.concatenate([row[:, None] * inv, col[:, None] * inv], axis=-1)
    cos, sin = jnp.cos(ang), jnp.sin(ang)
    reps = LANES // HEAD_DIM
    return (jnp.tile(jnp.concatenate([cos, cos], axis=-1), (1, reps)),
            jnp.tile(jnp.concatenate([-sin, sin], axis=-1), (1, reps)))


def _pick(n, pref):
    t = min(n, pref)
    while n % t:
        t //= 2
    return t


def kernel(x, c, ctx, c_ctx, w_ada, b_ada, norm1_g, w_in, q_norm_g, k_norm_g, dw_w, dw_b,
           conv_ln_g, conv_ln_b, w_out, norm2_g, ffn_w1, ffn_w3, ffn_w2, router_w, router_b,
           exp_w1, exp_w3, exp_w2, final_g):
    bsz, s, d = x.shape
    n_ctx = ctx.shape[1]
    depth = w_ada.shape[0]
    aw = N_HEADS * HEAD_DIM
    kw = N_KV_HEADS * HEAD_DIM

    half, q_cols, k_cols, head_rows = _qk_column_order()
    cols = np.concatenate([q_cols, k_cols, np.arange(aw + kw, w_in.shape[2])])
    w_in_p = w_in[:, :, cols].astype(BF16)
    gain = jnp.concatenate([jnp.tile(q_norm_g[:, half], (1, N_HEADS)) * (HEAD_DIM ** -0.5 * np.log2(np.e)),
                            jnp.tile(k_norm_g[:, half], (1, N_KV_HEADS))], axis=1)
    w_out_a = w_out[:, head_rows, :].astype(BF16)
    w_out_c = w_out[:, aw:, :].astype(BF16)
    blk = np.kron(np.eye(2 * LANES // HEAD_DIM), np.ones((HEAD_DIM, HEAD_DIM)))
    bd = jnp.asarray(blk, BF16)
    cos, sin = _rope_tables(s)

    pad = (-(bsz + 1)) % 8
    cvec = jnp.concatenate([c, c_ctx[None, :], jnp.zeros((pad, d), F32)], axis=0)
    mods = _modulation(cvec, w_ada, b_ada)

    tm_lat = _pick(s, 1024)
    ck = _pick(int(np.gcd(s, n_ctx)), 256)
    tq_lat = _pick(s, 512)
    tm_ctx = n_ctx
    tm_conv = _pick(s, 256)
    tm_ffn = _pick(s, 512)

    xc = ctx
    for l in range(depth):
        last = l == depth - 1
        m = [mods[l, :, i * d:(i + 1) * d] for i in range(6)]
        lat = [v[:bsz, None, :] for v in m]
        cx = [v[bsz:bsz + 1, None, :] for v in m]
        ng1 = norm1_g[l].reshape(1, d)
        ng2 = norm2_g[l].reshape(1, d)
        gl = gain[l].reshape(1, aw + kw)

        q, k, vt, y = _inproj(x, lat[0], lat[1], ng1, w_in_p[l], gl, bd, cos, sin, tm=tm_lat, ck=ck)
        qc, kc, vtc, yc = _inproj(xc, cx[0], cx[1], ng1, w_in_p[l], gl, bd, None, None,
                                  tm=tm_ctx, ck=ck)
        kc4 = kc.reshape(bsz, n_ctx // ck, ck, kw)
        k_all = jnp.concatenate([k.reshape(bsz, s // ck, ck, kw), kc4], axis=1)
        vt_all = jnp.concatenate([vt, vtc], axis=1)
        a_t = _attention(q, k_all, vt_all, tq=tq_lat)
        c_lat = _conformer_conv(y, dw_w[l], dw_b[l], conv_ln_g[l], conv_ln_b[l], tm=tm_conv)
        x = _outproj(a_t, c_lat, w_out_a[l], w_out_c[l], x, lat[2], tm=tm_ffn)
        if not last:
            a_tc = _attention(qc, kc4, vtc, tq=n_ctx)
            c_c = _conformer_conv(yc, dw_w[l], dw_b[l], conv_ln_g[l], conv_ln_b[l], tm=n_ctx)
            xc = _outproj(a_tc, c_c, w_out_a[l], w_out_c[l], xc, cx[2], tm=n_ctx)

        i = l // 2
        if l % 2 == 0:
            w1 = ffn_w1[i:i + 1].astype(BF16)
            w3 = ffn_w3[i:i + 1].astype(BF16)
            w2 = ffn_w2[i:i + 1].astype(BF16)
            router = None
        else:
            w1 = exp_w1[i].astype(BF16)
            w3 = exp_w3[i].astype(BF16)
            w2 = exp_w2[i].astype(BF16)
            w_r = jnp.zeros((d, LANES), F32).at[:, :N_EXPERTS].set(router_w[i])
            b_r = jnp.full((1, LANES), NEG_BIG, F32).at[0, :N_EXPERTS].set(router_b[i])
            router = (w_r, b_r)
        x = _ffn(x, lat[3], lat[4], lat[5], ng2, w1, w3, w2, router, tm=tm_ffn)
        if not last:
            xc = _ffn(xc, cx[3], cx[4], cx[5], ng2, w1, w3, w2, router, tm=n_ctx)
    return _final_norm(x, final_g, tm=tm_ffn)
```

```python
import functools

import numpy as np
import jax
import jax.numpy as jnp
from jax import lax
from jax.experimental import pallas as pl
from jax.experimental.pallas import tpu as pltpu

F32 = jnp.float32
BF16 = jnp.bfloat16

N_HEADS = 8
N_KV_HEADS = 2
HEAD_DIM = 64
GRID_W = 64
CONV_K = 31
N_EXPERTS = 8
ROPE_THETA = 10000.0
EPS = 1e-6

LANES = 128
SUBLANES = 8
BF16_ROWS = 16
VMEM_LIMIT = 56 * 1024 * 1024
NEG_BIG = -1e30


def _params(*sem):
    return pltpu.CompilerParams(dimension_semantics=sem, vmem_limit_bytes=VMEM_LIMIT)


def _silu(v):
    return v / (1.0 + jnp.exp(-v))


def _mod_kernel(c_ref, w_ref, b_ref, o_ref):
    c = c_ref[...]
    o_ref[0] = jnp.dot(_silu(c), w_ref[0], preferred_element_type=F32,
                       precision=lax.Precision.HIGHEST) + b_ref[0]


def _modulation(cvec, w_ada, b_ada):
    depth, d, n = w_ada.shape
    r = cvec.shape[0]
    tn = 1536
    return pl.pallas_call(
        _mod_kernel,
        out_shape=jax.ShapeDtypeStruct((depth, r, n), F32),
        grid=(depth, n // tn),
        in_specs=[
            pl.BlockSpec((r, d), lambda l, j: (0, 0)),
            pl.BlockSpec((1, d, tn), lambda l, j: (l, 0, j)),
            pl.BlockSpec((1, 1, tn), lambda l, j: (l, 0, j)),
        ],
        out_specs=pl.BlockSpec((1, r, tn), lambda l, j: (l, 0, j)),
        compiler_params=_params("parallel", "parallel"),
        name="modulation",
    )(cvec, w_ada, b_ada.reshape(depth, 1, n))


def _norm_mod(x, ng, sc, sh):
    ms = jnp.mean(x * x, axis=-1, keepdims=True)
    return (x * lax.rsqrt(ms + EPS)) * ng * (1.0 + sc) + sh


def _inproj_kernel(*refs, use_rope, ck):
    if use_rope:
        (x_ref, sh_ref, sc_ref, ng_ref, w_ref, gain_ref, bd_ref, cos_ref, sin_ref,
         q_ref, k_ref, vt_ref, y_ref) = refs
    else:
        (x_ref, sh_ref, sc_ref, ng_ref, w_ref, gain_ref, bd_ref,
         q_ref, k_ref, vt_ref, y_ref) = refs
    aw = N_HEADS * HEAD_DIM
    kw = N_KV_HEADS * HEAD_DIM
    qkw = aw + kw
    x = x_ref[0]
    tm = x.shape[0]
    h = _norm_mod(x, ng_ref[...], sc_ref[0], sh_ref[0]).astype(BF16)
    res = jnp.dot(h, w_ref[...], preferred_element_type=F32)

    qk = res[:, :qkw]
    sq = (qk * qk).astype(BF16)
    bd = bd_ref[...]
    parts = []
    for c0 in range(0, qkw, 2 * LANES):
        w = min(2 * LANES, qkw - c0)
        parts.append(jnp.dot(sq[:, c0:c0 + w], bd[:w, :w], preferred_element_type=F32))
    ss = jnp.concatenate(parts, axis=1)
    qk = qk * lax.rsqrt(ss * (1.0 / HEAD_DIM) + EPS) * gain_ref[...]

    if use_rope:
        cos = cos_ref[...]
        sin = sin_ref[...]
        lane = lax.broadcasted_iota(jnp.int32, (tm, LANES), 1)
        first = (lane % HEAD_DIM) < (HEAD_DIM // 2)
    outs = []
    for g in range(qkw // LANES):
        blk = qk[:, g * LANES:(g + 1) * LANES]
        if use_rope:
            half = HEAD_DIM // 2
            swapped = jnp.where(first, pltpu.roll(blk, LANES - half, 1), pltpu.roll(blk, half, 1))
            blk = blk * cos + swapped * sin
        outs.append(blk)
    q_ref[0] = jnp.concatenate(outs[:aw // LANES], axis=1).astype(BF16)
    k_ref[0] = outs[aw // LANES].astype(BF16)

    v = res[:, qkw:qkw + kw]
    for c in range(tm // ck):
        vt_ref[0, c] = v[c * ck:(c + 1) * ck, :].T.astype(BF16)

    cw = (res.shape[1] - qkw - kw) // 2
    a = res[:, qkw + kw:qkw + kw + cw]
    gt = res[:, qkw + kw + cw:]
    y_ref[0] = (a / (1.0 + jnp.exp(-gt))).astype(BF16)


def _inproj(x, sh, sc, ng, w, gain, bd, cos, sin, *, tm, ck):
    b, n, d = x.shape
    in_w = w.shape[1]
    aw = N_HEADS * HEAD_DIM
    kw = N_KV_HEADS * HEAD_DIM
    cw = (in_w - aw - 2 * kw) // 2
    use_rope = cos is not None
    bm = sh.shape[0]
    mod_map = (lambda bi, i: (bi, 0, 0)) if bm > 1 else (lambda bi, i: (0, 0, 0))
    const2 = lambda bi, i: (0, 0)
    in_specs = [
        pl.BlockSpec((1, tm, d), lambda bi, i: (bi, i, 0)),
        pl.BlockSpec((1, 1, d), mod_map),
        pl.BlockSpec((1, 1, d), mod_map),
        pl.BlockSpec((1, d), const2),
        pl.BlockSpec((d, in_w), const2),
        pl.BlockSpec((1, aw + kw), const2),
        pl.BlockSpec((2 * LANES, 2 * LANES), const2),
    ]
    args = [x, sh, sc, ng, w, gain, bd]
    if use_rope:
        in_specs += [pl.BlockSpec((tm, LANES), lambda bi, i: (i, 0))] * 2
        args += [cos, sin]
    out_shape = (
        jax.ShapeDtypeStruct((b, n, aw), BF16),
        jax.ShapeDtypeStruct((b, n, kw), BF16),
        jax.ShapeDtypeStruct((b, n // ck, kw, ck), BF16),
        jax.ShapeDtypeStruct((b, n, cw), BF16),
    )
    out_specs = (
        pl.BlockSpec((1, tm, aw), lambda bi, i: (bi, i, 0)),
        pl.BlockSpec((1, tm, kw), lambda bi, i: (bi, i, 0)),
        pl.BlockSpec((1, tm // ck, kw, ck), lambda bi, i: (bi, i, 0, 0)),
        pl.BlockSpec((1, tm, cw), lambda bi, i: (bi, i, 0)),
    )
    return pl.pallas_call(
        functools.partial(_inproj_kernel, use_rope=use_rope, ck=ck),
        out_shape=out_shape,
        grid=(b, n // tm),
        in_specs=in_specs,
        out_specs=out_specs,
        compiler_params=_params("parallel", "parallel"),
        name="inproj_lat" if use_rope else "inproj_ctx",
    )(*args)


def _attn_kernel(q_ref, k_ref, vt_ref, o_ref, m_ref, acc_ref, sa_ref, sb_ref, *, n_chunks):
    qblk = q_ref[0]
    ck = k_ref.shape[2]
    lane = lax.broadcasted_iota(jnp.int32, qblk.shape, 1)
    qms = [jnp.where((lane >= hh * HEAD_DIM) & (lane < (hh + 1) * HEAD_DIM), qblk, jnp.zeros_like(qblk))
           for hh in range(N_KV_HEADS)]
    m_ref[...] = jnp.full(m_ref.shape, NEG_BIG, F32)
    acc_ref[...] = jnp.zeros(acc_ref.shape, F32)

    def scores(c, s_ref):
        kc = k_ref[0, c]
        for hh in range(N_KV_HEADS):
            s_ref[hh] = lax.dot_general(kc, qms[hh], (((1,), (1,)), ((), ())),
                                        preferred_element_type=F32)

    def consume(c, s_ref):
        for hh in range(N_KV_HEADS):
            s = s_ref[hh]
            m_old = m_ref[hh]
            m_new = jnp.maximum(m_old, jnp.max(s, axis=0, keepdims=True))
            alpha = jnp.exp2(m_old - m_new)
            p = jnp.exp2(s - m_new).astype(BF16)
            vc = vt_ref[0, c, hh * HEAD_DIM:(hh + 1) * HEAD_DIM, :]
            vaug = jnp.concatenate([vc, jnp.ones((BF16_ROWS, ck), BF16)], axis=0)
            acc_ref[hh] = acc_ref[hh] * alpha + jnp.dot(vaug, p, preferred_element_type=F32)
            m_ref[hh] = m_new

    scores(0, sa_ref)
    n_pairs = (n_chunks - 1) // 2
    if n_pairs > 0:
        def body(j, carry):
            c = 2 * j
            scores(c + 1, sb_ref)
            consume(c, sa_ref)
            scores(c + 2, sa_ref)
            consume(c + 1, sb_ref)
            return carry
        lax.fori_loop(0, n_pairs, body, 0)
    if n_chunks - 2 * n_pairs == 1:
        consume(n_chunks - 1, sa_ref)
    else:
        scores(n_chunks - 1, sb_ref)
        consume(n_chunks - 2, sa_ref)
        consume(n_chunks - 1, sb_ref)

    for hh in range(N_KV_HEADS):
        acc = acc_ref[hh]
        o = acc[:HEAD_DIM] / acc[HEAD_DIM:HEAD_DIM + 1]
        o_ref[0, hh * HEAD_DIM:(hh + 1) * HEAD_DIM, :] = o.astype(BF16)


def _attention(q, k, vt, *, tq):
    b, n, aw = q.shape
    groups = aw // LANES
    _, nc, ck, kw = k.shape
    return pl.pallas_call(
        functools.partial(_attn_kernel, n_chunks=nc),
        out_shape=jax.ShapeDtypeStruct((b, aw, n), BF16),
        grid=(b, groups, n // tq),
        in_specs=[
            pl.BlockSpec((1, tq, LANES), lambda bi, j, qi: (bi, qi, j)),
            pl.BlockSpec((1, nc, ck, kw), lambda bi, j, qi: (bi, 0, 0, 0)),
            pl.BlockSpec((1, nc, kw, ck), lambda bi, j, qi: (bi, 0, 0, 0)),
        ],
        out_specs=pl.BlockSpec((1, LANES, tq), lambda bi, j, qi: (bi, j, qi)),
        scratch_shapes=[pltpu.VMEM((N_KV_HEADS, 1, tq), F32),
                        pltpu.VMEM((N_KV_HEADS, HEAD_DIM + BF16_ROWS, tq), F32),
                        pltpu.VMEM((N_KV_HEADS, ck, tq), F32),
                        pltpu.VMEM((N_KV_HEADS, ck, tq), F32)],
        compiler_params=_params("parallel", "parallel", "parallel"),
        name="attention_lat" if nc > 1 else "attention_ctx",
    )(q, k, vt)


CONV_HALO = 16
CONV_ROWS = 32


def _conv_kernel(y_ref, yp_ref, yn_ref, w_ref, b_ref, g_ref, be_ref, o_ref, buf_ref, sh_ref):
    i = pl.program_id(1)
    last = pl.num_programs(1) - 1
    tm = y_ref.shape[1]
    prev = yp_ref[0].astype(F32)
    nxt = yn_ref[0].astype(F32)
    buf_ref[0:CONV_HALO, :] = jnp.where(i > 0, prev, jnp.zeros_like(prev))
    buf_ref[CONV_HALO:CONV_HALO + tm, :] = y_ref[0].astype(F32)
    buf_ref[CONV_HALO + tm:, :] = jnp.where(i < last, nxt, jnp.zeros_like(nxt))
    span = sh_ref.shape[1]
    for res in range(1, SUBLANES):
        sh_ref[res - 1] = buf_ref[res:res + span, :]
    w = w_ref[...]
    off = CONV_HALO - CONV_K // 2
    for r in range(tm // CONV_ROWS):
        r0 = r * CONV_ROWS
        acc = jnp.zeros((CONV_ROWS, w.shape[1]), F32)
        for k in range(CONV_K):
            res = (k + off) % SUBLANES
            base = r0 + k + off - res
            if res == 0:
                tap = buf_ref[base:base + CONV_ROWS, :]
            else:
                tap = sh_ref[res - 1, base:base + CONV_ROWS, :]
            acc = acc + w[k:k + 1, :] * tap
        acc = acc + b_ref[...]
        mu = jnp.mean(acc, axis=-1, keepdims=True)
        cen = acc - mu
        var = jnp.mean(cen * cen, axis=-1, keepdims=True)
        z = cen * lax.rsqrt(var + EPS) * g_ref[...] + be_ref[...]
        o_ref[0, r0:r0 + CONV_ROWS, :] = _silu(z).astype(BF16)


def _conformer_conv(y, w_dw, b_dw, ln_g, ln_b, *, tm):
    b, n, cw = y.shape
    hb = tm // CONV_HALO
    nh = n // CONV_HALO
    vec = pl.BlockSpec((1, cw), lambda bi, i: (0, 0))
    return pl.pallas_call(
        _conv_kernel,
        out_shape=jax.ShapeDtypeStruct((b, n, cw), BF16),
        grid=(b, n // tm),
        in_specs=[
            pl.BlockSpec((1, tm, cw), lambda bi, i: (bi, i, 0)),
            pl.BlockSpec((1, CONV_HALO, cw), lambda bi, i: (bi, jnp.maximum(i * hb - 1, 0), 0)),
            pl.BlockSpec((1, CONV_HALO, cw), lambda bi, i: (bi, jnp.minimum((i + 1) * hb, nh - 1), 0)),
            pl.BlockSpec((CONV_K, cw), lambda bi, i: (0, 0)),
            vec, vec, vec,
        ],
        out_specs=pl.BlockSpec((1, tm, cw), lambda bi, i: (bi, i, 0)),
        scratch_shapes=[pltpu.VMEM((tm + 2 * CONV_HALO, cw), F32),
                        pltpu.VMEM((SUBLANES - 1, tm + 2 * CONV_HALO - SUBLANES, cw), F32)],
        compiler_params=_params("parallel", "parallel"),
        name="conformer_conv",
    )(y, y, y, w_dw, b_dw.reshape(1, cw), ln_g.reshape(1, cw), ln_b.reshape(1, cw))


def _outproj_kernel(at_ref, c_ref, wa_ref, wc_ref, x_ref, g_ref, o_ref):
    o = lax.dot_general(at_ref[0], wa_ref[...], (((0,), (0,)), ((), ())),
                        preferred_element_type=F32)
    o = o + jnp.dot(c_ref[0], wc_ref[...], preferred_element_type=F32)
    o_ref[0] = x_ref[0] + g_ref[0] * o


def _outproj(a_t, c, w_a, w_c, x, gate, *, tm):
    b, n, d = x.shape
    aw = a_t.shape[1]
    cw = c.shape[2]
    bm = gate.shape[0]
    mod_map = (lambda bi, i: (bi, 0, 0)) if bm > 1 else (lambda bi, i: (0, 0, 0))
    return pl.pallas_call(
        _outproj_kernel,
        out_shape=jax.ShapeDtypeStruct((b, n, d), F32),
        grid=(b, n // tm),
        in_specs=[
            pl.BlockSpec((1, aw, tm), lambda bi, i: (bi, 0, i)),
            pl.BlockSpec((1, tm, cw), lambda bi, i: (bi, i, 0)),
            pl.BlockSpec((aw, d), lambda bi, i: (0, 0)),
            pl.BlockSpec((cw, d), lambda bi, i: (0, 0)),
            pl.BlockSpec((1, tm, d), lambda bi, i: (bi, i, 0)),
            pl.BlockSpec((1, 1, d), mod_map),
        ],
        out_specs=pl.BlockSpec((1, tm, d), lambda bi, i: (bi, i, 0)),
        compiler_params=_params("parallel", "parallel"),
        name="outproj",
    )(a_t, c, w_a, w_c, x, gate)


def _swiglu(h, w1_ref, w3_ref, w2_ref, ff_chunk):
    d_ff = w1_ref.shape[2]
    f = None
    for c0 in range(0, d_ff, ff_chunk):
        a = jnp.dot(h, w1_ref[0, :, c0:c0 + ff_chunk], preferred_element_type=F32)
        b = jnp.dot(h, w3_ref[0, :, c0:c0 + ff_chunk], preferred_element_type=F32)
        z = (_silu(a) * b).astype(BF16)
        part = jnp.dot(z, w2_ref[0, c0:c0 + ff_chunk, :], preferred_element_type=F32)
        f = part if f is None else f + part
    return f


def _ffn_kernel(x_ref, sh_ref, sc_ref, g_ref, ng_ref, w1_ref, w3_ref, w2_ref, o_ref, *, ff_chunk):
    h = _norm_mod(x_ref[0], ng_ref[...], sc_ref[0], sh_ref[0]).astype(BF16)
    o_ref[0] = x_ref[0] + g_ref[0] * _swiglu(h, w1_ref, w3_ref, w2_ref, ff_chunk)


def _ffn(x, sh, sc, gate, ng, w1, w3, w2, *, tm):
    b, n, d = x.shape
    d_ff = w1.shape[2]
    bm = sh.shape[0]
    mod_map = (lambda bi, i: (bi, 0, 0)) if bm > 1 else (lambda bi, i: (0, 0, 0))
    mod = pl.BlockSpec((1, 1, d), mod_map)
    return pl.pallas_call(
        functools.partial(_ffn_kernel, ff_chunk=d_ff // 2),
        out_shape=jax.ShapeDtypeStruct((b, n, d), F32),
        grid=(b, n // tm),
        in_specs=[
            pl.BlockSpec((1, tm, d), lambda bi, i: (bi, i, 0)),
            mod, mod, mod,
            pl.BlockSpec((1, d), lambda bi, i: (0, 0)),
            pl.BlockSpec((1, d, d_ff), lambda bi, i: (0, 0, 0)),
            pl.BlockSpec((1, d, d_ff), lambda bi, i: (0, 0, 0)),
            pl.BlockSpec((1, d_ff, d), lambda bi, i: (0, 0, 0)),
        ],
        out_specs=pl.BlockSpec((1, tm, d), lambda bi, i: (bi, i, 0)),
        compiler_params=_params("parallel", "parallel"),
        name="dense_ffn",
    )(x, sh, sc, gate, ng, w1, w3, w2)


META_G1, META_G2, META_E1, META_E2, META_R1, META_R2 = range(6)
TOK_ROWS = 8


def _to_token_tiles(ref, val, lead=()):
    n = val.shape[0]
    for c in range(TOK_ROWS):
        ref[lead + (pl.ds(c, n, stride=TOK_ROWS), slice(None))] = val[:, c * LANES:(c + 1) * LANES]


def _from_token_tiles(ref, n, lead=()):
    return jnp.concatenate([ref[lead + (pl.ds(c, n, stride=TOK_ROWS), slice(None))]
                            for c in range(TOK_ROWS)], axis=1)


def _router_kernel(x_ref, sh_ref, sc_ref, ng_ref, wr_ref, br_ref, tri_ref,
                   h_ref, meta_ref, cnt_ref, carry_ref):
    i = pl.program_id(0)

    @pl.when(i == 0)
    def _():
        carry_ref[...] = jnp.zeros(carry_ref.shape, F32)

    h = _norm_mod(x_ref[...], ng_ref[...], sc_ref[0], sh_ref[0])
    _to_token_tiles(h_ref, h)
    logits = jnp.dot(h, wr_ref[...], preferred_element_type=F32,
                     precision=lax.Precision.HIGHEST) + br_ref[...]
    lane = lax.broadcasted_iota(jnp.int32, logits.shape, 1)
    t1 = jnp.max(logits, axis=-1, keepdims=True)
    i1 = jnp.min(jnp.where(logits == t1, lane, LANES), axis=-1, keepdims=True)
    rest = jnp.where(lane == i1, NEG_BIG, logits)
    t2 = jnp.max(rest, axis=-1, keepdims=True)
    i2 = jnp.min(jnp.where(rest == t2, lane, LANES), axis=-1, keepdims=True)
    e2 = jnp.exp(t2 - t1)
    den = 1.0 + e2
    onehot = jnp.where((lane == i1) | (lane == i2), 1.0, 0.0)
    before = carry_ref[...] + jnp.dot(tri_ref[...], onehot.astype(BF16), preferred_element_type=F32)
    r1 = jnp.sum(jnp.where(lane == i1, before, 0.0), axis=-1, keepdims=True)
    r2 = jnp.sum(jnp.where(lane == i2, before, 0.0), axis=-1, keepdims=True)
    meta = jnp.zeros(logits.shape, F32)
    for col, val in ((META_G1, 1.0 / den), (META_G2, e2 / den), (META_E1, i1.astype(F32)),
                     (META_E2, i2.astype(F32)), (META_R1, r1), (META_R2, r2)):
        meta = jnp.where(lane == col, val, meta)
    meta_ref[...] = meta
    carry_ref[...] = carry_ref[...] + jnp.sum(onehot, axis=0, keepdims=True)
    cnt_ref[...] = carry_ref[...]


def _router(x2, sh, sc, ng, w_r, b_r, tri, *, tm, rows_per_mod):
    n, d = x2.shape
    per = rows_per_mod // tm
    mod_map = lambda i: (i // per, 0, 0)
    const = lambda i: (0, 0)
    return pl.pallas_call(
        _router_kernel,
        out_shape=(jax.ShapeDtypeStruct((n * TOK_ROWS, LANES), F32),
                   jax.ShapeDtypeStruct((n, LANES), F32),
                   jax.ShapeDtypeStruct((1, LANES), F32)),
        grid=(n // tm,),
        in_specs=[
            pl.BlockSpec((tm, d), lambda i: (i, 0)),
            pl.BlockSpec((1, 1, d), mod_map),
            pl.BlockSpec((1, 1, d), mod_map),
            pl.BlockSpec((1, d), const),
            pl.BlockSpec((d, LANES), const),
            pl.BlockSpec((1, LANES), const),
            pl.BlockSpec((tm, tm), const),
        ],
        out_specs=(pl.BlockSpec((tm * TOK_ROWS, LANES), lambda i: (i, 0)),
                   pl.BlockSpec((tm, LANES), lambda i: (i, 0)),
                   pl.BlockSpec((1, LANES), const)),
        scratch_shapes=[pltpu.VMEM((1, LANES), F32)],
        compiler_params=_params("arbitrary"),
        name="moe_router",
    )(x2, sh, sc, ng, w_r, b_r, tri)


def _tok(ref, t, count=1):
    start = t * TOK_ROWS
    if not isinstance(t, int):
        start = pl.multiple_of(start, TOK_ROWS)
    return ref.at[pl.ds(start, count * TOK_ROWS)]


def _row_copy(src_ref, src_row, dst_ref, dst_row, sem):
    return pltpu.make_async_copy(_tok(src_ref, src_row), _tok(dst_ref, dst_row), sem)


ROW_UNROLL = 8


def _scatter_kernel(tail_ref, pos_ref, h_ref, hs_ref, zero_ref, sem):
    i = pl.program_id(0)
    tm = h_ref.shape[0] // TOK_ROWS
    tile = zero_ref.shape[0] // TOK_ROWS

    @pl.when(i == 0)
    def _():
        zero_ref[...] = jnp.zeros(zero_ref.shape, F32)
        n_tiles = hs_ref.shape[0] // zero_ref.shape[0]
        n_used = tail_ref[N_EXPERTS]

        def zero_tile(first_row):
            return pltpu.make_async_copy(zero_ref, _tok(hs_ref, first_row, tile), sem)

        for e in range(N_EXPERTS):
            zero_tile(tail_ref[e]).start()
        lax.fori_loop(n_used, n_tiles, lambda t, c: (zero_tile(t * tile).start(), c)[1], 0)
        for e in range(N_EXPERTS):
            zero_tile(tail_ref[e]).wait()
        lax.fori_loop(n_used, n_tiles, lambda t, c: (zero_tile(t * tile).wait(), c)[1], 0)

    def issue(r, carry):
        for k in range(2):
            _row_copy(h_ref, r, hs_ref, pos_ref[0, 0, k * tm + r], sem).start()
        return carry

    lax.fori_loop(0, tm, issue, 0, unroll=ROW_UNROLL)

    def drain(r, carry):
        _row_copy(h_ref, 0, hs_ref, 0, sem).wait()
        return carry

    lax.fori_loop(0, 2 * tm, drain, 0, unroll=ROW_UNROLL)


def _scatter(tails, pos, h, *, tm, n_rows, tile):
    n = h.shape[0] // TOK_ROWS
    return pl.pallas_call(
        _scatter_kernel,
        out_shape=jax.ShapeDtypeStruct((n_rows * TOK_ROWS, LANES), F32),
        grid_spec=pltpu.PrefetchScalarGridSpec(
            num_scalar_prefetch=1,
            grid=(n // tm,),
            in_specs=[
                pl.BlockSpec((1, 1, 2 * tm), lambda i, tails: (i, 0, 0), memory_space=pltpu.SMEM),
                pl.BlockSpec((tm * TOK_ROWS, LANES), lambda i, tails: (i, 0)),
            ],
            out_specs=pl.BlockSpec(memory_space=pl.ANY),
            scratch_shapes=[pltpu.VMEM((tile * TOK_ROWS, LANES), F32), pltpu.SemaphoreType.DMA],
        ),
        compiler_params=_params("arbitrary"),
        name="moe_scatter",
    )(tails, pos, h)


def _expert_kernel(te_ref, nu_ref, hs_ref, w1_ref, w3_ref, w2_ref, y_ref, *, ff_chunk):
    used = pl.program_id(0) < nu_ref[0]

    @pl.when(used)
    def _():
        tile = hs_ref.shape[0] // TOK_ROWS
        h = _from_token_tiles(hs_ref, tile).astype(BF16)
        _to_token_tiles(y_ref, _swiglu(h, w1_ref, w3_ref, w2_ref, ff_chunk))

    @pl.when(jnp.logical_not(used))
    def _():
        y_ref[...] = jnp.zeros(y_ref.shape, F32)


def _experts(tile_e, n_used, hs, w1, w3, w2, *, tile):
    n_rows = hs.shape[0] // TOK_ROWS
    d = w1.shape[1]
    d_ff = w1.shape[2]
    row_map = lambda t, te, nu: (t, 0)
    w_map = lambda t, te, nu: (te[t], 0, 0)
    return pl.pallas_call(
        functools.partial(_expert_kernel, ff_chunk=d_ff // 2),
        out_shape=jax.ShapeDtypeStruct((n_rows * TOK_ROWS, LANES), F32),
        grid_spec=pltpu.PrefetchScalarGridSpec(
            num_scalar_prefetch=2,
            grid=(n_rows // tile,),
            in_specs=[
                pl.BlockSpec((tile * TOK_ROWS, LANES), row_map),
                pl.BlockSpec((1, d, d_ff), w_map),
                pl.BlockSpec((1, d, d_ff), w_map),
                pl.BlockSpec((1, d_ff, d), w_map),
            ],
            out_specs=pl.BlockSpec((tile * TOK_ROWS, LANES), row_map),
        ),
        compiler_params=_params("arbitrary"),
        name="moe_experts",
    )(tile_e, n_used, hs, w1, w3, w2)


def _combine_kernel(pos_ref, x_ref, g_ref, meta_ref, y_ref, o_ref, buf_ref, sem):
    tm = x_ref.shape[0]

    def issue(r, carry):
        for k in range(2):
            _row_copy(y_ref, pos_ref[0, 0, k * tm + r], buf_ref.at[k], r, sem).start()
        return carry

    lax.fori_loop(0, tm, issue, 0, unroll=ROW_UNROLL)

    def drain(r, carry):
        _row_copy(y_ref, 0, buf_ref.at[0], 0, sem).wait()
        return carry

    lax.fori_loop(0, 2 * tm, drain, 0, unroll=ROW_UNROLL)
    meta = meta_ref[...]
    g1 = meta[:, META_G1:META_G1 + 1]
    g2 = meta[:, META_G2:META_G2 + 1]
    y1 = _from_token_tiles(buf_ref, tm, (0,))
    y2 = _from_token_tiles(buf_ref, tm, (1,))
    o_ref[...] = x_ref[...] + g_ref[0] * (g1 * y1 + g2 * y2)


def _combine(pos, x2, gate, meta, y, *, tm, rows_per_mod):
    n, d = x2.shape
    per = rows_per_mod // tm
    return pl.pallas_call(
        _combine_kernel,
        out_shape=jax.ShapeDtypeStruct((n, d), F32),
        grid=(n // tm,),
        in_specs=[
            pl.BlockSpec((1, 1, 2 * tm), lambda i: (i, 0, 0), memory_space=pltpu.SMEM),
            pl.BlockSpec((tm, d), lambda i: (i, 0)),
            pl.BlockSpec((1, 1, d), lambda i: (i // per, 0, 0)),
            pl.BlockSpec((tm, LANES), lambda i: (i, 0)),
            pl.BlockSpec(memory_space=pl.ANY),
        ],
        out_specs=pl.BlockSpec((tm, d), lambda i: (i, 0)),
        scratch_shapes=[pltpu.VMEM((2, tm * TOK_ROWS, LANES), F32), pltpu.SemaphoreType.DMA],
        compiler_params=_params("arbitrary"),
        name="moe_combine",
    )(pos, x2, gate, meta, y)


def _moe(x, sh, sc, gate, ng, w1, w3, w2, w_r, b_r, tri, *, tm, tile):
    b, n, d = x.shape
    rows = b * n
    rows_per_mod = n if sh.shape[0] > 1 else rows
    x2 = x.reshape(rows, d)
    h, meta, cnt = _router(x2, sh, sc, ng, w_r, b_r, tri, tm=tm, rows_per_mod=rows_per_mod)

    counts = cnt[0, :N_EXPERTS].astype(jnp.int32)
    padded = (counts + tile - 1) // tile * tile
    ends = jnp.cumsum(padded)
    starts = ends - padded
    experts = meta[:, META_E1:META_E2 + 1].astype(jnp.int32)
    ranks = meta[:, META_R1:META_R2 + 1].astype(jnp.int32)
    pos = starts[experts] + ranks
    pos = pos.reshape(rows // tm, tm, 2).transpose(0, 2, 1).reshape(rows // tm, 1, 2 * tm)
    n_rows = 2 * rows + N_EXPERTS * tile
    n_tiles = n_rows // tile
    n_used = (ends[-1] // tile).astype(jnp.int32)
    t = jnp.minimum(jnp.arange(n_tiles, dtype=jnp.int32), n_used - 1)
    tile_e = jnp.minimum(jnp.searchsorted(ends, t * tile, side="right"), N_EXPERTS - 1).astype(jnp.int32)
    busiest = jnp.argmax(counts)
    tails = jnp.where(padded > 0, ends - tile, ends[busiest] - tile).astype(jnp.int32)
    tails = jnp.concatenate([tails, n_used.reshape(1)])

    hs = _scatter(tails, pos, h, tm=tm, n_rows=n_rows, tile=tile)
    y = _experts(tile_e, n_used.reshape(1), hs, w1, w3, w2, tile=tile)
    out = _combine(pos, x2, gate, meta, y, tm=tm, rows_per_mod=rows_per_mod)
    return out.reshape(b, n, d)


def _final_kernel(x_ref, g_ref, o_ref):
    x = x_ref[0]
    ms = jnp.mean(x * x, axis=-1, keepdims=True)
    o_ref[0] = x * lax.rsqrt(ms + EPS) * g_ref[...]


def _final_norm(x, g, *, tm):
    b, n, d = x.shape
    return pl.pallas_call(
        _final_kernel,
        out_shape=jax.ShapeDtypeStruct((b, n, d), F32),
        grid=(b, n // tm),
        in_specs=[pl.BlockSpec((1, tm, d), lambda bi, i: (bi, i, 0)),
                  pl.BlockSpec((1, d), lambda bi, i: (0, 0))],
        out_specs=pl.BlockSpec((1, tm, d), lambda bi, i: (bi, i, 0)),
        compiler_params=_params("parallel", "parallel"),
        name="final_norm",
    )(x, g.reshape(1, d))


def _qk_column_order():
    half = np.concatenate([np.arange(0, HEAD_DIM, 2), np.arange(1, HEAD_DIM, 2)])
    per_group = N_HEADS // N_KV_HEADS
    heads = [h for j in range(per_group) for h in range(j, N_HEADS, per_group)]
    q_cols = np.concatenate([h * HEAD_DIM + half for h in heads])
    k_cols = N_HEADS * HEAD_DIM + np.concatenate([g * HEAD_DIM + half for g in range(N_KV_HEADS)])
    head_rows = np.concatenate([h * HEAD_DIM + np.arange(HEAD_DIM) for h in heads])
    return half, q_cols, k_cols, head_rows


def _rope_tables(n_tokens):
    rows = n_tokens // GRID_W
    row = jnp.repeat(jnp.arange(rows), GRID_W).astype(F32)
    col = jnp.tile(jnp.arange(GRID_W), rows).astype(F32)
    half = HEAD_DIM // 2
    inv = ROPE_THETA ** (-jnp.arange(0, half, 2, dtype=F32) / half)
    ang = jnp.concatenate([row[:, None] * inv, col[:, None] * inv], axis=-1)
    cos, sin = jnp.cos(ang), jnp.sin(ang)
    reps = LANES // HEAD_DIM
    return (jnp.tile(jnp.concatenate([cos, cos], axis=-1), (1, reps)),
            jnp.tile(jnp.concatenate([-sin, sin], axis=-1), (1, reps)))


MOE_ROWS = 512
MOE_TILE = 512


def _tri(n):
    return jnp.asarray(np.tril(np.ones((n, n), np.float32), -1), BF16)


def _pick(n, pref):
    t = min(n, pref)
    while n % t:
        t //= 2
    return t


def kernel(x, c, ctx, c_ctx, w_ada, b_ada, norm1_g, w_in, q_norm_g, k_norm_g, dw_w, dw_b,
           conv_ln_g, conv_ln_b, w_out, norm2_g, ffn_w1, ffn_w3, ffn_w2, router_w, router_b,
           exp_w1, exp_w3, exp_w2, final_g):
    bsz, s, d = x.shape
    n_ctx = ctx.shape[1]
    depth = w_ada.shape[0]
    aw = N_HEADS * HEAD_DIM
    kw = N_KV_HEADS * HEAD_DIM

    half, q_cols, k_cols, head_rows = _qk_column_order()
    cols = np.concatenate([q_cols, k_cols, np.arange(aw + kw, w_in.shape[2])])
    w_in_p = w_in[:, :, cols].astype(BF16)
    gain = jnp.concatenate([jnp.tile(q_norm_g[:, half], (1, N_HEADS)) * (HEAD_DIM ** -0.5 * np.log2(np.e)),
                            jnp.tile(k_norm_g[:, half], (1, N_KV_HEADS))], axis=1)
    w_out_a = w_out[:, head_rows, :].astype(BF16)
    w_out_c = w_out[:, aw:, :].astype(BF16)
    blk = np.kron(np.eye(2 * LANES // HEAD_DIM), np.ones((HEAD_DIM, HEAD_DIM)))
    bd = jnp.asarray(blk, BF16)
    cos, sin = _rope_tables(s)

    pad = (-(bsz + 1)) % 8
    cvec = jnp.concatenate([c, c_ctx[None, :], jnp.zeros((pad, d), F32)], axis=0)
    mods = _modulation(cvec, w_ada, b_ada)

    tm_lat = _pick(s, 1024)
    ck = _pick(int(np.gcd(s, n_ctx)), 256)
    tq_lat = _pick(s, 512)
    tm_ctx = n_ctx
    tm_conv = _pick(s, 256)
    tm_ffn = _pick(s, 512)

    xc = ctx
    for l in range(depth):
        last = l == depth - 1
        m = [mods[l, :, i * d:(i + 1) * d] for i in range(6)]
        lat = [v[:bsz, None, :] for v in m]
        cx = [v[bsz:bsz + 1, None, :] for v in m]
        ng1 = norm1_g[l].reshape(1, d)
        ng2 = norm2_g[l].reshape(1, d)
        gl = gain[l].reshape(1, aw + kw)

        q, k, vt, y = _inproj(x, lat[0], lat[1], ng1, w_in_p[l], gl, bd, cos, sin, tm=tm_lat, ck=ck)
        qc, kc, vtc, yc = _inproj(xc, cx[0], cx[1], ng1, w_in_p[l], gl, bd, None, None,
                                  tm=tm_ctx, ck=ck)
        kc4 = kc.reshape(bsz, n_ctx // ck, ck, kw)
        k_all = jnp.concatenate([k.reshape(bsz, s // ck, ck, kw), kc4], axis=1)
        vt_all = jnp.concatenate([vt, vtc], axis=1)
        a_t = _attention(q, k_all, vt_all, tq=tq_lat)
        c_lat = _conformer_conv(y, dw_w[l], dw_b[l], conv_ln_g[l], conv_ln_b[l], tm=tm_conv)
        x = _outproj(a_t, c_lat, w_out_a[l], w_out_c[l], x, lat[2], tm=tm_ffn)
        if not last:
            a_tc = _attention(qc, kc4, vtc, tq=n_ctx)
            c_c = _conformer_conv(yc, dw_w[l], dw_b[l], conv_ln_g[l], conv_ln_b[l], tm=n_ctx)
            xc = _outproj(a_tc, c_c, w_out_a[l], w_out_c[l], xc, cx[2], tm=n_ctx)

        i = l // 2
        if l % 2 == 0:
            w1 = ffn_w1[i:i + 1].astype(BF16)
            w3 = ffn_w3[i:i + 1].astype(BF16)
            w2 = ffn_w2[i:i + 1].astype(BF16)
            x = _ffn(x, lat[3], lat[4], lat[5], ng2, w1, w3, w2, tm=tm_ffn)
            if not last:
                xc = _ffn(xc, cx[3], cx[4], cx[5], ng2, w1, w3, w2, tm=n_ctx)
        else:
            w1 = exp_w1[i].astype(BF16)
            w3 = exp_w3[i].astype(BF16)
            w2 = exp_w2[i].astype(BF16)
            w_r = jnp.zeros((d, LANES), F32).at[:, :N_EXPERTS].set(router_w[i])
            b_r = jnp.full((1, LANES), NEG_BIG, F32).at[0, :N_EXPERTS].set(router_b[i])
            tm_l = _pick(s, MOE_ROWS)
            x = _moe(x, lat[3], lat[4], lat[5], ng2, w1, w3, w2, w_r, b_r, _tri(tm_l),
                     tm=tm_l, tile=MOE_TILE)
            if not last:
                tm_c = _pick(bsz * n_ctx, MOE_ROWS)
                xc = _moe(xc, cx[3], cx[4], cx[5], ng2, w1, w3, w2, w_r, b_r, _tri(tm_c),
                          tm=tm_c, tile=MOE_TILE)
    return _final_norm(x, final_g, tm=tm_ffn)
```

```python
import functools

import numpy as np
import jax
import jax.numpy as jnp
from jax import lax
from jax.experimental import pallas as pl
from jax.experimental.pallas import tpu as pltpu

F32 = jnp.float32
BF16 = jnp.bfloat16

N_HEADS = 8
N_KV_HEADS = 2
HEAD_DIM = 64
GRID_W = 64
CONV_K = 31
N_EXPERTS = 8
ROPE_THETA = 10000.0
EPS = 1e-6

LANES = 128
SUBLANES = 8
MXU_COLS = 256
BF16_ROWS = 16
VMEM_LIMIT = 56 * 1024 * 1024
NEG_BIG = -1e30


def _params(*sem):
    return pltpu.CompilerParams(dimension_semantics=sem, vmem_limit_bytes=VMEM_LIMIT)


def _silu(v):
    return v / (1.0 + jnp.exp(-v))


def _mod_kernel(c_ref, w_ref, b_ref, o_ref):
    c = c_ref[...]
    o_ref[0] = jnp.dot(_silu(c), w_ref[0], preferred_element_type=F32,
                       precision=lax.Precision.HIGHEST) + b_ref[0]


def _modulation(cvec, w_ada, b_ada):
    depth, d, n = w_ada.shape
    r = cvec.shape[0]
    tn = 1536
    return pl.pallas_call(
        _mod_kernel,
        out_shape=jax.ShapeDtypeStruct((depth, r, n), F32),
        grid=(depth, n // tn),
        in_specs=[
            pl.BlockSpec((r, d), lambda l, j: (0, 0)),
            pl.BlockSpec((1, d, tn), lambda l, j: (l, 0, j)),
            pl.BlockSpec((1, 1, tn), lambda l, j: (l, 0, j)),
        ],
        out_specs=pl.BlockSpec((1, r, tn), lambda l, j: (l, 0, j)),
        compiler_params=_params("parallel", "parallel"),
        name="modulation",
    )(cvec, w_ada, b_ada.reshape(depth, 1, n))


def _norm_mod(x, ng, sc, sh):
    ms = jnp.mean(x * x, axis=-1, keepdims=True)
    return (x * lax.rsqrt(ms + EPS)) * ng * (1.0 + sc) + sh


def _inproj_kernel(*refs, use_rope):
    if use_rope:
        (x_ref, sh_ref, sc_ref, ng_ref, w_ref, gain_ref, bd_ref, cos_ref, sin_ref,
         q_ref, k_ref, vt_ref, y_ref) = refs
    else:
        (x_ref, sh_ref, sc_ref, ng_ref, w_ref, gain_ref, bd_ref,
         q_ref, k_ref, vt_ref, y_ref) = refs
    aw = N_HEADS * HEAD_DIM
    kw = N_KV_HEADS * HEAD_DIM
    qkw = aw + kw
    x = x_ref[0]
    tm = x.shape[0]
    h = _norm_mod(x, ng_ref[...], sc_ref[0], sh_ref[0]).astype(BF16)
    res = jnp.dot(h, w_ref[...], preferred_element_type=F32)

    qk = res[:, :qkw]
    sq = (qk * qk).astype(BF16)
    bd = bd_ref[...]
    parts = []
    for c0 in range(0, qkw, 2 * LANES):
        w = min(2 * LANES, qkw - c0)
        parts.append(jnp.dot(sq[:, c0:c0 + w], bd[:w, :w], preferred_element_type=F32))
    ss = jnp.concatenate(parts, axis=1)
    qk = qk * lax.rsqrt(ss * (1.0 / HEAD_DIM) + EPS) * gain_ref[...]

    if use_rope:
        cos = cos_ref[...]
        sin = sin_ref[...]
        lane = lax.broadcasted_iota(jnp.int32, (tm, LANES), 1)
        first = (lane % HEAD_DIM) < (HEAD_DIM // 2)
    outs = []
    for g in range(qkw // LANES):
        blk = qk[:, g * LANES:(g + 1) * LANES]
        if use_rope:
            half = HEAD_DIM // 2
            swapped = jnp.where(first, pltpu.roll(blk, LANES - half, 1), pltpu.roll(blk, half, 1))
            blk = blk * cos + swapped * sin
        outs.append(blk)
    q_ref[0] = jnp.concatenate(outs[:aw // LANES], axis=1).astype(BF16)
    k_ref[0] = outs[aw // LANES].astype(BF16)

    v = res[:, qkw:qkw + kw]
    vt_ref[0] = v.T.astype(BF16)

    cw = (res.shape[1] - qkw - kw) // 2
    a = res[:, qkw + kw:qkw + kw + cw]
    gt = res[:, qkw + kw + cw:]
    y_ref[0] = (a / (1.0 + jnp.exp(-gt))).astype(BF16)


def _inproj(x, sh, sc, ng, w, gain, bd, cos, sin, *, tm):
    b, n, d = x.shape
    in_w = w.shape[1]
    aw = N_HEADS * HEAD_DIM
    kw = N_KV_HEADS * HEAD_DIM
    cw = (in_w - aw - 2 * kw) // 2
    use_rope = cos is not None
    bm = sh.shape[0]
    mod_map = (lambda bi, i: (bi, 0, 0)) if bm > 1 else (lambda bi, i: (0, 0, 0))
    const2 = lambda bi, i: (0, 0)
    in_specs = [
        pl.BlockSpec((1, tm, d), lambda bi, i: (bi, i, 0)),
        pl.BlockSpec((1, 1, d), mod_map),
        pl.BlockSpec((1, 1, d), mod_map),
        pl.BlockSpec((1, d), const2),
        pl.BlockSpec((d, in_w), const2),
        pl.BlockSpec((1, aw + kw), const2),
        pl.BlockSpec((2 * LANES, 2 * LANES), const2),
    ]
    args = [x, sh, sc, ng, w, gain, bd]
    if use_rope:
        in_specs += [pl.BlockSpec((tm, LANES), lambda bi, i: (i, 0))] * 2
        args += [cos, sin]
    out_shape = (
        jax.ShapeDtypeStruct((b, n, aw), BF16),
        jax.ShapeDtypeStruct((b, n, kw), BF16),
        jax.ShapeDtypeStruct((b, kw, n), BF16),
        jax.ShapeDtypeStruct((b, n, cw), BF16),
    )
    out_specs = (
        pl.BlockSpec((1, tm, aw), lambda bi, i: (bi, i, 0)),
        pl.BlockSpec((1, tm, kw), lambda bi, i: (bi, i, 0)),
        pl.BlockSpec((1, kw, tm), lambda bi, i: (bi, 0, i)),
        pl.BlockSpec((1, tm, cw), lambda bi, i: (bi, i, 0)),
    )
    return pl.pallas_call(
        functools.partial(_inproj_kernel, use_rope=use_rope),
        out_shape=out_shape,
        grid=(b, n // tm),
        in_specs=in_specs,
        out_specs=out_specs,
        compiler_params=_params("parallel", "parallel"),
        name="inproj_lat" if use_rope else "inproj_ctx",
    )(*args)


def _attn_kernel(q_ref, k_ref, vt_ref, o_ref, m_ref, acc_ref, sa_ref, sb_ref, ma_ref, mb_ref):
    qblk = q_ref[0]
    n_keys = k_ref.shape[1]
    ck_max = sa_ref.shape[1]
    chunks = [(c0, min(ck_max, n_keys - c0)) for c0 in range(0, n_keys, ck_max)]
    lane = lax.broadcasted_iota(jnp.int32, qblk.shape, 1)
    qms = [jnp.where((lane >= hh * HEAD_DIM) & (lane < (hh + 1) * HEAD_DIM), qblk, jnp.zeros_like(qblk))
           for hh in range(N_KV_HEADS)]
    m_ref[...] = jnp.full(m_ref.shape, NEG_BIG, F32)
    acc_ref[...] = jnp.zeros(acc_ref.shape, F32)

    tq = qblk.shape[0]
    strips = [slice(j, min(j + MXU_COLS, tq)) for j in range(0, tq, MXU_COLS)]

    def scores(chunk, s_ref, mc_ref):
        c0, ck = chunk
        kc = k_ref[0, c0:c0 + ck, :]
        for hh in range(N_KV_HEADS):
            for sl in strips:
                s = lax.dot_general(kc, qms[hh][sl, :], (((1,), (1,)), ((), ())),
                                    preferred_element_type=F32)
                s_ref[hh, :ck, sl] = s
                mc_ref[hh, :, sl] = jnp.max(s, axis=0, keepdims=True)

    def consume(chunk, s_ref, mc_ref):
        c0, ck = chunk
        for hh in range(N_KV_HEADS):
            vc = vt_ref[0, hh * HEAD_DIM:(hh + 1) * HEAD_DIM, c0:c0 + ck]
            vaug = jnp.concatenate([vc, jnp.ones((BF16_ROWS, ck), BF16)], axis=0)
            for sl in strips:
                m_old = m_ref[hh, :, sl]
                m_new = jnp.maximum(m_old, mc_ref[hh, :, sl])
                alpha = jnp.exp2(m_old - m_new)
                p = jnp.exp2(s_ref[hh, :ck, sl] - m_new).astype(BF16)
                acc_ref[hh, :, sl] = (acc_ref[hh, :, sl] * alpha
                                      + jnp.dot(vaug, p, preferred_element_type=F32))
                m_ref[hh, :, sl] = m_new

    bufs = ((sa_ref, ma_ref), (sb_ref, mb_ref))
    scores(chunks[0], *bufs[0])
    for c, chunk in enumerate(chunks):
        if c + 1 < len(chunks):
            scores(chunks[c + 1], *bufs[(c + 1) % 2])
        consume(chunk, *bufs[c % 2])

    for hh in range(N_KV_HEADS):
        acc = acc_ref[hh]
        o = acc[:HEAD_DIM] / acc[HEAD_DIM:HEAD_DIM + 1]
        o_ref[0, hh * HEAD_DIM:(hh + 1) * HEAD_DIM, :] = o.astype(BF16)


def _attention(q, k, vt, *, tq, ck):
    b, n, aw = q.shape
    groups = aw // LANES
    _, n_keys, kw = k.shape
    ck = min(ck, n_keys)
    return pl.pallas_call(
        _attn_kernel,
        out_shape=jax.ShapeDtypeStruct((b, aw, n), BF16),
        grid=(b, groups, n // tq),
        in_specs=[
            pl.BlockSpec((1, tq, LANES), lambda bi, j, qi: (bi, qi, j)),
            pl.BlockSpec((1, n_keys, kw), lambda bi, j, qi: (bi, 0, 0)),
            pl.BlockSpec((1, kw, n_keys), lambda bi, j, qi: (bi, 0, 0)),
        ],
        out_specs=pl.BlockSpec((1, LANES, tq), lambda bi, j, qi: (bi, j, qi)),
        scratch_shapes=[pltpu.VMEM((N_KV_HEADS, 1, tq), F32),
                        pltpu.VMEM((N_KV_HEADS, HEAD_DIM + BF16_ROWS, tq), F32),
                        pltpu.VMEM((N_KV_HEADS, ck, tq), F32),
                        pltpu.VMEM((N_KV_HEADS, ck, tq), F32),
                        pltpu.VMEM((N_KV_HEADS, 1, tq), F32),
                        pltpu.VMEM((N_KV_HEADS, 1, tq), F32)],
        compiler_params=_params("parallel", "parallel", "parallel"),
        name="attention_lat" if n_keys > n else "attention_ctx",
    )(q, k, vt)


CONV_HALO = 16
CONV_ROWS = 32


def _conv_kernel(y_ref, yp_ref, yn_ref, w_ref, b_ref, g_ref, be_ref, o_ref, buf_ref, sh_ref):
    i = pl.program_id(1)
    last = pl.num_programs(1) - 1
    tm = y_ref.shape[1]
    prev = yp_ref[0].astype(F32)
    nxt = yn_ref[0].astype(F32)
    buf_ref[0:CONV_HALO, :] = jnp.where(i > 0, prev, jnp.zeros_like(prev))
    buf_ref[CONV_HALO:CONV_HALO + tm, :] = y_ref[0].astype(F32)
    buf_ref[CONV_HALO + tm:, :] = jnp.where(i < last, nxt, jnp.zeros_like(nxt))
    span = sh_ref.shape[1]
    for res in range(1, SUBLANES):
        sh_ref[res - 1] = buf_ref[res:res + span, :]
    w = w_ref[...]
    off = CONV_HALO - CONV_K // 2
    for r in range(tm // CONV_ROWS):
        r0 = r * CONV_ROWS
        acc = jnp.zeros((CONV_ROWS, w.shape[1]), F32)
        for k in range(CONV_K):
            res = (k + off) % SUBLANES
            base = r0 + k + off - res
            if res == 0:
                tap = buf_ref[base:base + CONV_ROWS, :]
            else:
                tap = sh_ref[res - 1, base:base + CONV_ROWS, :]
            acc = acc + w[k:k + 1, :] * tap
        acc = acc + b_ref[...]
        mu = jnp.mean(acc, axis=-1, keepdims=True)
        cen = acc - mu
        var = jnp.mean(cen * cen, axis=-1, keepdims=True)
        z = cen * lax.rsqrt(var + EPS) * g_ref[...] + be_ref[...]
        o_ref[0, r0:r0 + CONV_ROWS, :] = _silu(z).astype(BF16)


def _conformer_conv(y, w_dw, b_dw, ln_g, ln_b, *, tm):
    b, n, cw = y.shape
    hb = tm // CONV_HALO
    nh = n // CONV_HALO
    vec = pl.BlockSpec((1, cw), lambda bi, i: (0, 0))
    return pl.pallas_call(
        _conv_kernel,
        out_shape=jax.ShapeDtypeStruct((b, n, cw), BF16),
        grid=(b, n // tm),
        in_specs=[
            pl.BlockSpec((1, tm, cw), lambda bi, i: (bi, i, 0)),
            pl.BlockSpec((1, CONV_HALO, cw), lambda bi, i: (bi, jnp.maximum(i * hb - 1, 0), 0)),
            pl.BlockSpec((1, CONV_HALO, cw), lambda bi, i: (bi, jnp.minimum((i + 1) * hb, nh - 1), 0)),
            pl.BlockSpec((CONV_K, cw), lambda bi, i: (0, 0)),
            vec, vec, vec,
        ],
        out_specs=pl.BlockSpec((1, tm, cw), lambda bi, i: (bi, i, 0)),
        scratch_shapes=[pltpu.VMEM((tm + 2 * CONV_HALO, cw), F32),
                        pltpu.VMEM((SUBLANES - 1, tm + 2 * CONV_HALO - SUBLANES, cw), F32)],
        compiler_params=_params("parallel", "parallel"),
        name="conformer_conv",
    )(y, y, y, w_dw, b_dw.reshape(1, cw), ln_g.reshape(1, cw), ln_b.reshape(1, cw))


def _outproj_kernel(at_ref, c_ref, wa_ref, wc_ref, x_ref, g_ref, o_ref):
    o = lax.dot_general(at_ref[0], wa_ref[...], (((0,), (0,)), ((), ())),
                        preferred_element_type=F32)
    o = o + jnp.dot(c_ref[0], wc_ref[...], preferred_element_type=F32)
    o_ref[0] = x_ref[0] + g_ref[0] * o


def _outproj(a_t, c, w_a, w_c, x, gate, *, tm):
    b, n, d = x.shape
    aw = a_t.shape[1]
    cw = c.shape[2]
    bm = gate.shape[0]
    mod_map = (lambda bi, i: (bi, 0, 0)) if bm > 1 else (lambda bi, i: (0, 0, 0))
    return pl.pallas_call(
        _outproj_kernel,
        out_shape=jax.ShapeDtypeStruct((b, n, d), F32),
        grid=(b, n // tm),
        in_specs=[
            pl.BlockSpec((1, aw, tm), lambda bi, i: (bi, 0, i)),
            pl.BlockSpec((1, tm, cw), lambda bi, i: (bi, i, 0)),
            pl.BlockSpec((aw, d), lambda bi, i: (0, 0)),
            pl.BlockSpec((cw, d), lambda bi, i: (0, 0)),
            pl.BlockSpec((1, tm, d), lambda bi, i: (bi, i, 0)),
            pl.BlockSpec((1, 1, d), mod_map),
        ],
        out_specs=pl.BlockSpec((1, tm, d), lambda bi, i: (bi, i, 0)),
        compiler_params=_params("parallel", "parallel"),
        name="outproj",
    )(a_t, c, w_a, w_c, x, gate)


def _swiglu(h, w1_ref, w3_ref, w2_ref, ff_chunk):
    d_ff = w1_ref.shape[2]
    f = None
    for c0 in range(0, d_ff, ff_chunk):
        a = jnp.dot(h, w1_ref[0, :, c0:c0 + ff_chunk], preferred_element_type=F32)
        b = jnp.dot(h, w3_ref[0, :, c0:c0 + ff_chunk], preferred_element_type=F32)
        z = (_silu(a) * b).astype(BF16)
        part = jnp.dot(z, w2_ref[0, c0:c0 + ff_chunk, :], preferred_element_type=F32)
        f = part if f is None else f + part
    return f


def _ffn_kernel(x_ref, sh_ref, sc_ref, g_ref, ng_ref, w1_ref, w3_ref, w2_ref, o_ref, *, ff_chunk):
    h = _norm_mod(x_ref[0], ng_ref[...], sc_ref[0], sh_ref[0]).astype(BF16)
    o_ref[0] = x_ref[0] + g_ref[0] * _swiglu(h, w1_ref, w3_ref, w2_ref, ff_chunk)


def _ffn(x, sh, sc, gate, ng, w1, w3, w2, *, tm):
    b, n, d = x.shape
    d_ff = w1.shape[2]
    bm = sh.shape[0]
    mod_map = (lambda bi, i: (bi, 0, 0)) if bm > 1 else (lambda bi, i: (0, 0, 0))
    mod = pl.BlockSpec((1, 1, d), mod_map)
    return pl.pallas_call(
        functools.partial(_ffn_kernel, ff_chunk=d_ff // 2),
        out_shape=jax.ShapeDtypeStruct((b, n, d), F32),
        grid=(b, n // tm),
        in_specs=[
            pl.BlockSpec((1, tm, d), lambda bi, i: (bi, i, 0)),
            mod, mod, mod,
            pl.BlockSpec((1, d), lambda bi, i: (0, 0)),
            pl.BlockSpec((1, d, d_ff), lambda bi, i: (0, 0, 0)),
            pl.BlockSpec((1, d, d_ff), lambda bi, i: (0, 0, 0)),
            pl.BlockSpec((1, d_ff, d), lambda bi, i: (0, 0, 0)),
        ],
        out_specs=pl.BlockSpec((1, tm, d), lambda bi, i: (bi, i, 0)),
        compiler_params=_params("parallel", "parallel"),
        name="dense_ffn",
    )(x, sh, sc, gate, ng, w1, w3, w2)


META_G1, META_G2, META_E1, META_E2, META_R1, META_R2 = range(6)
TOK_ROWS = 8


def _to_token_tiles(ref, val, lead=()):
    n = val.shape[0]
    for c in range(TOK_ROWS):
        ref[lead + (pl.ds(c, n, stride=TOK_ROWS), slice(None))] = val[:, c * LANES:(c + 1) * LANES]


def _from_token_tiles(ref, n, lead=()):
    return jnp.concatenate([ref[lead + (pl.ds(c, n, stride=TOK_ROWS), slice(None))]
                            for c in range(TOK_ROWS)], axis=1)


def _router_kernel(x_ref, sh_ref, sc_ref, ng_ref, wr_ref, br_ref, tri_ref,
                   h_ref, meta_ref, cnt_ref, carry_ref):
    i = pl.program_id(0)

    @pl.when(i == 0)
    def _():
        carry_ref[...] = jnp.zeros(carry_ref.shape, F32)

    h = _norm_mod(x_ref[...], ng_ref[...], sc_ref[0], sh_ref[0])
    _to_token_tiles(h_ref, h)
    logits = jnp.dot(h, wr_ref[...], preferred_element_type=F32,
                     precision=lax.Precision.HIGHEST) + br_ref[...]
    lane = lax.broadcasted_iota(jnp.int32, logits.shape, 1)
    t1 = jnp.max(logits, axis=-1, keepdims=True)
    i1 = jnp.min(jnp.where(logits == t1, lane, LANES), axis=-1, keepdims=True)
    rest = jnp.where(lane == i1, NEG_BIG, logits)
    t2 = jnp.max(rest, axis=-1, keepdims=True)
    i2 = jnp.min(jnp.where(rest == t2, lane, LANES), axis=-1, keepdims=True)
    e2 = jnp.exp(t2 - t1)
    den = 1.0 + e2
    onehot = jnp.where((lane == i1) | (lane == i2), 1.0, 0.0)
    before = carry_ref[...] + jnp.dot(tri_ref[...], onehot.astype(BF16), preferred_element_type=F32)
    r1 = jnp.sum(jnp.where(lane == i1, before, 0.0), axis=-1, keepdims=True)
    r2 = jnp.sum(jnp.where(lane == i2, before, 0.0), axis=-1, keepdims=True)
    meta = jnp.zeros(logits.shape, F32)
    for col, val in ((META_G1, 1.0 / den), (META_G2, e2 / den), (META_E1, i1.astype(F32)),
                     (META_E2, i2.astype(F32)), (META_R1, r1), (META_R2, r2)):
        meta = jnp.where(lane == col, val, meta)
    meta_ref[...] = meta
    carry_ref[...] = carry_ref[...] + jnp.sum(onehot, axis=0, keepdims=True)
    cnt_ref[...] = carry_ref[...]


def _router(x2, sh, sc, ng, w_r, b_r, tri, *, tm, rows_per_mod):
    n, d = x2.shape
    per = rows_per_mod // tm
    mod_map = lambda i: (i // per, 0, 0)
    const = lambda i: (0, 0)
    return pl.pallas_call(
        _router_kernel,
        out_shape=(jax.ShapeDtypeStruct((n * TOK_ROWS, LANES), F32),
                   jax.ShapeDtypeStruct((n, LANES), F32),
                   jax.ShapeDtypeStruct((1, LANES), F32)),
        grid=(n // tm,),
        in_specs=[
            pl.BlockSpec((tm, d), lambda i: (i, 0)),
            pl.BlockSpec((1, 1, d), mod_map),
            pl.BlockSpec((1, 1, d), mod_map),
            pl.BlockSpec((1, d), const),
            pl.BlockSpec((d, LANES), const),
            pl.BlockSpec((1, LANES), const),
            pl.BlockSpec((tm, tm), const),
        ],
        out_specs=(pl.BlockSpec((tm * TOK_ROWS, LANES), lambda i: (i, 0)),
                   pl.BlockSpec((tm, LANES), lambda i: (i, 0)),
                   pl.BlockSpec((1, LANES), const)),
        scratch_shapes=[pltpu.VMEM((1, LANES), F32)],
        compiler_params=_params("arbitrary"),
        name="moe_router",
    )(x2, sh, sc, ng, w_r, b_r, tri)


def _tok(ref, t, count=1):
    start = t * TOK_ROWS
    if not isinstance(t, int):
        start = pl.multiple_of(start, TOK_ROWS)
    return ref.at[pl.ds(start, count * TOK_ROWS)]


def _row_copy(src_ref, src_row, dst_ref, dst_row, sem):
    return pltpu.make_async_copy(_tok(src_ref, src_row), _tok(dst_ref, dst_row), sem)


ROW_UNROLL = 8


def _scatter_kernel(tail_ref, pos_ref, h_ref, hs_ref, zero_ref, sem):
    i = pl.program_id(0)
    tm = h_ref.shape[0] // TOK_ROWS
    tile = zero_ref.shape[0] // TOK_ROWS

    @pl.when(i == 0)
    def _():
        zero_ref[...] = jnp.zeros(zero_ref.shape, F32)
        n_tiles = hs_ref.shape[0] // zero_ref.shape[0]
        n_used = tail_ref[N_EXPERTS]

        def zero_tile(first_row):
            return pltpu.make_async_copy(zero_ref, _tok(hs_ref, first_row, tile), sem)

        for e in range(N_EXPERTS):
            zero_tile(tail_ref[e]).start()
        lax.fori_loop(n_used, n_tiles, lambda t, c: (zero_tile(t * tile).start(), c)[1], 0)
        for e in range(N_EXPERTS):
            zero_tile(tail_ref[e]).wait()
        lax.fori_loop(n_used, n_tiles, lambda t, c: (zero_tile(t * tile).wait(), c)[1], 0)

    def issue(r, carry):
        for k in range(2):
            _row_copy(h_ref, r, hs_ref, pos_ref[0, 0, k * tm + r], sem).start(priority=k)
        return carry

    lax.fori_loop(0, tm, issue, 0, unroll=ROW_UNROLL)

    def drain(r, carry):
        _row_copy(h_ref, 0, hs_ref, 0, sem).wait()
        return carry

    lax.fori_loop(0, 2 * tm, drain, 0, unroll=ROW_UNROLL)


def _scatter(tails, pos, h, *, tm, n_rows, tile):
    n = h.shape[0] // TOK_ROWS
    return pl.pallas_call(
        _scatter_kernel,
        out_shape=jax.ShapeDtypeStruct((n_rows * TOK_ROWS, LANES), F32),
        grid_spec=pltpu.PrefetchScalarGridSpec(
            num_scalar_prefetch=1,
            grid=(n // tm,),
            in_specs=[
                pl.BlockSpec((1, 1, 2 * tm), lambda i, tails: (i, 0, 0), memory_space=pltpu.SMEM),
                pl.BlockSpec((tm * TOK_ROWS, LANES), lambda i, tails: (i, 0)),
            ],
            out_specs=pl.BlockSpec(memory_space=pl.ANY),
            scratch_shapes=[pltpu.VMEM((tile * TOK_ROWS, LANES), F32), pltpu.SemaphoreType.DMA],
        ),
        compiler_params=_params("arbitrary"),
        name="moe_scatter",
    )(tails, pos, h)


def _expert_kernel(te_ref, nu_ref, hs_ref, w1_ref, w3_ref, w2_ref, y_ref, *, ff_chunk):
    used = pl.program_id(0) < nu_ref[0]

    @pl.when(used)
    def _():
        tile = hs_ref.shape[0] // TOK_ROWS
        h = _from_token_tiles(hs_ref, tile).astype(BF16)
        _to_token_tiles(y_ref, _swiglu(h, w1_ref, w3_ref, w2_ref, ff_chunk))

    @pl.when(jnp.logical_not(used))
    def _():
        y_ref[...] = jnp.zeros(y_ref.shape, F32)


def _experts(tile_e, n_used, hs, w1, w3, w2, *, tile):
    n_rows = hs.shape[0] // TOK_ROWS
    d = w1.shape[1]
    d_ff = w1.shape[2]
    row_map = lambda t, te, nu: (t, 0)
    w_map = lambda t, te, nu: (te[t], 0, 0)
    return pl.pallas_call(
        functools.partial(_expert_kernel, ff_chunk=d_ff // 2),
        out_shape=jax.ShapeDtypeStruct((n_rows * TOK_ROWS, LANES), F32),
        grid_spec=pltpu.PrefetchScalarGridSpec(
            num_scalar_prefetch=2,
            grid=(n_rows // tile,),
            in_specs=[
                pl.BlockSpec((tile * TOK_ROWS, LANES), row_map),
                pl.BlockSpec((1, d, d_ff), w_map),
                pl.BlockSpec((1, d, d_ff), w_map),
                pl.BlockSpec((1, d_ff, d), w_map),
            ],
            out_specs=pl.BlockSpec((tile * TOK_ROWS, LANES), row_map),
        ),
        compiler_params=_params("arbitrary"),
        name="moe_experts",
    )(tile_e, n_used, hs, w1, w3, w2)


def _combine_kernel(pos_ref, x_ref, g_ref, meta_ref, y_ref, *rest, final_norm):
    fg_ref = rest[0] if final_norm else None
    o_ref, buf_ref, sem = rest[-3:]
    tm = x_ref.shape[0]

    def issue(r, carry):
        for k in range(2):
            _row_copy(y_ref, pos_ref[0, 0, k * tm + r], buf_ref.at[k], r, sem).start(priority=k)
        return carry

    lax.fori_loop(0, tm, issue, 0, unroll=ROW_UNROLL)

    def drain(r, carry):
        _row_copy(y_ref, 0, buf_ref.at[0], 0, sem).wait()
        return carry

    lax.fori_loop(0, 2 * tm, drain, 0, unroll=ROW_UNROLL)
    meta = meta_ref[...]
    g1 = meta[:, META_G1:META_G1 + 1]
    g2 = meta[:, META_G2:META_G2 + 1]
    y1 = _from_token_tiles(buf_ref, tm, (0,))
    y2 = _from_token_tiles(buf_ref, tm, (1,))
    out = x_ref[...] + g_ref[0] * (g1 * y1 + g2 * y2)
    if final_norm:
        ms = jnp.mean(out * out, axis=-1, keepdims=True)
        out = out * lax.rsqrt(ms + EPS) * fg_ref[...]
    o_ref[...] = out


def _combine(pos, x2, gate, meta, y, final_g=None, *, tm, rows_per_mod):
    n, d = x2.shape
    per = rows_per_mod // tm
    in_specs = [
        pl.BlockSpec((1, 1, 2 * tm), lambda i: (i, 0, 0), memory_space=pltpu.SMEM),
        pl.BlockSpec((tm, d), lambda i: (i, 0)),
        pl.BlockSpec((1, 1, d), lambda i: (i // per, 0, 0)),
        pl.BlockSpec((tm, LANES), lambda i: (i, 0)),
        pl.BlockSpec(memory_space=pl.ANY),
    ]
    args = [pos, x2, gate, meta, y]
    if final_g is not None:
        in_specs.append(pl.BlockSpec((1, d), lambda i: (0, 0)))
        args.append(final_g.reshape(1, d))
    return pl.pallas_call(
        functools.partial(_combine_kernel, final_norm=final_g is not None),
        out_shape=jax.ShapeDtypeStruct((n, d), F32),
        grid=(n // tm,),
        in_specs=in_specs,
        out_specs=pl.BlockSpec((tm, d), lambda i: (i, 0)),
        scratch_shapes=[pltpu.VMEM((2, tm * TOK_ROWS, LANES), F32), pltpu.SemaphoreType.DMA],
        compiler_params=_params("arbitrary"),
        name="moe_combine",
    )(*args)


def _moe(x, sh, sc, gate, ng, w1, w3, w2, w_r, b_r, tri, final_g=None, *, tm, tile):
    b, n, d = x.shape
    rows = b * n
    rows_per_mod = n if sh.shape[0] > 1 else rows
    x2 = x.reshape(rows, d)
    h, meta, cnt = _router(x2, sh, sc, ng, w_r, b_r, tri, tm=tm, rows_per_mod=rows_per_mod)

    counts = cnt[0, :N_EXPERTS].astype(jnp.int32)
    padded = (counts + tile - 1) // tile * tile
    ends = jnp.cumsum(padded)
    starts = ends - padded
    experts = meta[:, META_E1:META_E2 + 1].astype(jnp.int32)
    ranks = meta[:, META_R1:META_R2 + 1].astype(jnp.int32)
    pos = starts[experts] + ranks
    pos = pos.reshape(rows // tm, tm, 2).transpose(0, 2, 1).reshape(rows // tm, 1, 2 * tm)
    n_rows = 2 * rows + N_EXPERTS * tile
    n_tiles = n_rows // tile
    n_used = (ends[-1] // tile).astype(jnp.int32)
    t = jnp.minimum(jnp.arange(n_tiles, dtype=jnp.int32), n_used - 1)
    tile_e = jnp.minimum(jnp.searchsorted(ends, t * tile, side="right"), N_EXPERTS - 1).astype(jnp.int32)
    busiest = jnp.argmax(counts)
    tails = jnp.where(padded > 0, ends - tile, ends[busiest] - tile).astype(jnp.int32)
    tails = jnp.concatenate([tails, n_used.reshape(1)])

    hs = _scatter(tails, pos, h, tm=tm, n_rows=n_rows, tile=tile)
    y = _experts(tile_e, n_used.reshape(1), hs, w1, w3, w2, tile=tile)
    out = _combine(pos, x2, gate, meta, y, final_g, tm=tm, rows_per_mod=rows_per_mod)
    return out.reshape(b, n, d)


def _final_kernel(x_ref, g_ref, o_ref):
    x = x_ref[0]
    ms = jnp.mean(x * x, axis=-1, keepdims=True)
    o_ref[0] = x * lax.rsqrt(ms + EPS) * g_ref[...]


def _final_norm(x, g, *, tm):
    b, n, d = x.shape
    return pl.pallas_call(
        _final_kernel,
        out_shape=jax.ShapeDtypeStruct((b, n, d), F32),
        grid=(b, n // tm),
        in_specs=[pl.BlockSpec((1, tm, d), lambda bi, i: (bi, i, 0)),
                  pl.BlockSpec((1, d), lambda bi, i: (0, 0))],
        out_specs=pl.BlockSpec((1, tm, d), lambda bi, i: (bi, i, 0)),
        compiler_params=_params("parallel", "parallel"),
        name="final_norm",
    )(x, g.reshape(1, d))


def _qk_column_order():
    half = np.concatenate([np.arange(0, HEAD_DIM, 2), np.arange(1, HEAD_DIM, 2)])
    per_group = N_HEADS // N_KV_HEADS
    heads = [h for j in range(per_group) for h in range(j, N_HEADS, per_group)]
    q_cols = np.concatenate([h * HEAD_DIM + half for h in heads])
    k_cols = N_HEADS * HEAD_DIM + np.concatenate([g * HEAD_DIM + half for g in range(N_KV_HEADS)])
    head_rows = np.concatenate([h * HEAD_DIM + np.arange(HEAD_DIM) for h in heads])
    return half, q_cols, k_cols, head_rows


def _rope_tables(n_tokens):
    rows = n_tokens // GRID_W
    row = jnp.repeat(jnp.arange(rows), GRID_W).astype(F32)
    col = jnp.tile(jnp.arange(GRID_W), rows).astype(F32)
    half = HEAD_DIM // 2
    inv = ROPE_THETA ** (-jnp.arange(0, half, 2, dtype=F32) / half)
    ang = jnp.concatenate([row[:, None] * inv, col[:, None] * inv], axis=-1)
    cos, sin = jnp.cos(ang), jnp.sin(ang)
    reps = LANES // HEAD_DIM
    return (jnp.tile(jnp.concatenate([cos, cos], axis=-1), (1, reps)),
            jnp.tile(jnp.concatenate([-sin, sin], axis=-1), (1, reps)))


ATTN_CHUNK = 512
MOE_ROWS = 512
MOE_TILE = 512


def _tri(n):
    return jnp.asarray(np.tril(np.ones((n, n), np.float32), -1), BF16)


def _pick(n, pref):
    t = min(n, pref)
    while n % t:
        t //= 2
    return t


def kernel(x, c, ctx, c_ctx, w_ada, b_ada, norm1_g, w_in, q_norm_g, k_norm_g, dw_w, dw_b,
           conv_ln_g, conv_ln_b, w_out, norm2_g, ffn_w1, ffn_w3, ffn_w2, router_w, router_b,
           exp_w1, exp_w3, exp_w2, final_g):
    bsz, s, d = x.shape
    n_ctx = ctx.shape[1]
    depth = w_ada.shape[0]
    aw = N_HEADS * HEAD_DIM
    kw = N_KV_HEADS * HEAD_DIM

    half, q_cols, k_cols, head_rows = _qk_column_order()
    cols = np.concatenate([q_cols, k_cols, np.arange(aw + kw, w_in.shape[2])])
    w_in_p = w_in[:, :, cols].astype(BF16)
    gain = jnp.concatenate([jnp.tile(q_norm_g[:, half], (1, N_HEADS)) * (HEAD_DIM ** -0.5 * np.log2(np.e)),
                            jnp.tile(k_norm_g[:, half], (1, N_KV_HEADS))], axis=1)
    w_out_a = w_out[:, head_rows, :].astype(BF16)
    w_out_c = w_out[:, aw:, :].astype(BF16)
    blk = np.kron(np.eye(2 * LANES // HEAD_DIM), np.ones((HEAD_DIM, HEAD_DIM)))
    bd = jnp.asarray(blk, BF16)
    cos, sin = _rope_tables(s)

    pad = (-(bsz + 1)) % 8
    cvec = jnp.concatenate([c, c_ctx[None, :], jnp.zeros((pad, d), F32)], axis=0)
    mods = _modulation(cvec, w_ada, b_ada)

    tm_lat = _pick(s, 1024)
    tq_lat = _pick(s, 512)
    tm_ctx = n_ctx
    tm_conv = _pick(s, 256)
    tm_ffn = _pick(s, 512)

    xc = ctx
    for l in range(depth):
        last = l == depth - 1
        m = [mods[l, :, i * d:(i + 1) * d] for i in range(6)]
        lat = [v[:bsz, None, :] for v in m]
        cx = [v[bsz:bsz + 1, None, :] for v in m]
        ng1 = norm1_g[l].reshape(1, d)
        ng2 = norm2_g[l].reshape(1, d)
        gl = gain[l].reshape(1, aw + kw)

        q, k, vt, y = _inproj(x, lat[0], lat[1], ng1, w_in_p[l], gl, bd, cos, sin, tm=tm_lat)
        qc, kc, vtc, yc = _inproj(xc, cx[0], cx[1], ng1, w_in_p[l], gl, bd, None, None,
                                  tm=tm_ctx)
        k_all = jnp.concatenate([k, kc], axis=1)
        vt_all = jnp.concatenate([vt, vtc], axis=2)
        a_t = _attention(q, k_all, vt_all, tq=tq_lat, ck=ATTN_CHUNK)
        c_lat = _conformer_conv(y, dw_w[l], dw_b[l], conv_ln_g[l], conv_ln_b[l], tm=tm_conv)
        x = _outproj(a_t, c_lat, w_out_a[l], w_out_c[l], x, lat[2], tm=tm_ffn)
        if not last:
            a_tc = _attention(qc, kc, vtc, tq=n_ctx, ck=ATTN_CHUNK)
            c_c = _conformer_conv(yc, dw_w[l], dw_b[l], conv_ln_g[l], conv_ln_b[l], tm=n_ctx)
            xc = _outproj(a_tc, c_c, w_out_a[l], w_out_c[l], xc, cx[2], tm=n_ctx)

        i = l // 2
        if l % 2 == 0:
            w1 = ffn_w1[i:i + 1].astype(BF16)
            w3 = ffn_w3[i:i + 1].astype(BF16)
            w2 = ffn_w2[i:i + 1].astype(BF16)
            x = _ffn(x, lat[3], lat[4], lat[5], ng2, w1, w3, w2, tm=tm_ffn)
            if not last:
                xc = _ffn(xc, cx[3], cx[4], cx[5], ng2, w1, w3, w2, tm=n_ctx)
        else:
            w1 = exp_w1[i].astype(BF16)
            w3 = exp_w3[i].astype(BF16)
            w2 = exp_w2[i].astype(BF16)
            w_r = jnp.zeros((d, LANES), F32).at[:, :N_EXPERTS].set(router_w[i])
            b_r = jnp.full((1, LANES), NEG_BIG, F32).at[0, :N_EXPERTS].set(router_b[i])
            tm_l = _pick(s, MOE_ROWS)
            x = _moe(x, lat[3], lat[4], lat[5], ng2, w1, w3, w2, w_r, b_r, _tri(tm_l),
                     final_g if last else None, tm=tm_l, tile=MOE_TILE)
            if not last:
                tm_c = _pick(bsz * n_ctx, MOE_ROWS)
                xc = _moe(xc, cx[3], cx[4], cx[5], ng2, w1, w3, w2, w_r, b_r, _tri(tm_c),
                          tm=tm_c, tile=MOE_TILE)
    if depth % 2 == 0:
        return x
    return _final_norm(x, final_g, tm=tm_ffn)
```

```python
import functools

import numpy as np
import jax
import jax.numpy as jnp
from jax import lax
from jax.experimental import pallas as pl
from jax.experimental.pallas import tpu as pltpu

F32 = jnp.float32
BF16 = jnp.bfloat16

N_HEADS = 8
N_KV_HEADS = 2
HEAD_DIM = 64
GRID_W = 64
CONV_K = 31
N_EXPERTS = 8
ROPE_THETA = 10000.0
EPS = 1e-6

LANES = 128
SUBLANES = 8
MXU_COLS = 256
BF16_ROWS = 16
VMEM_LIMIT = 56 * 1024 * 1024
NEG_BIG = -1e30


def _params(*sem):
    return pltpu.CompilerParams(dimension_semantics=sem, vmem_limit_bytes=VMEM_LIMIT)


def _silu(v):
    return v / (1.0 + jnp.exp(-v))


def _mod_kernel(c_ref, w_ref, b_ref, o_ref):
    c = c_ref[...]
    o_ref[0] = jnp.dot(_silu(c), w_ref[0], preferred_element_type=F32,
                       precision=lax.Precision.HIGHEST) + b_ref[0]


def _modulation(cvec, w_ada, b_ada):
    depth, d, n = w_ada.shape
    r = cvec.shape[0]
    tn = 1536
    return pl.pallas_call(
        _mod_kernel,
        out_shape=jax.ShapeDtypeStruct((depth, r, n), F32),
        grid=(depth, n // tn),
        in_specs=[
            pl.BlockSpec((r, d), lambda l, j: (0, 0)),
            pl.BlockSpec((1, d, tn), lambda l, j: (l, 0, j)),
            pl.BlockSpec((1, 1, tn), lambda l, j: (l, 0, j)),
        ],
        out_specs=pl.BlockSpec((1, r, tn), lambda l, j: (l, 0, j)),
        compiler_params=_params("parallel", "parallel"),
        name="modulation",
    )(cvec, w_ada, b_ada.reshape(depth, 1, n))


def _norm_mod(x, ng, sc, sh):
    ms = jnp.mean(x * x, axis=-1, keepdims=True)
    return (x * lax.rsqrt(ms + EPS)) * ng * (1.0 + sc) + sh


def _inproj_kernel(*refs, use_rope):
    if use_rope:
        (x_ref, sh_ref, sc_ref, ng_ref, w_ref, gain_ref, bd_ref, cos_ref, sin_ref,
         q_ref, k_ref, vt_ref, y_ref) = refs
    else:
        (x_ref, sh_ref, sc_ref, ng_ref, w_ref, gain_ref, bd_ref,
         q_ref, k_ref, vt_ref, y_ref) = refs
    aw = N_HEADS * HEAD_DIM
    kw = N_KV_HEADS * HEAD_DIM
    qkw = aw + kw
    x = x_ref[0]
    tm = x.shape[0]
    h = _norm_mod(x, ng_ref[...], sc_ref[0], sh_ref[0]).astype(BF16)
    res = jnp.dot(h, w_ref[...], preferred_element_type=F32)

    qk = res[:, :qkw]
    sq = (qk * qk).astype(BF16)
    bd = bd_ref[...]
    parts = []
    for c0 in range(0, qkw, 2 * LANES):
        w = min(2 * LANES, qkw - c0)
        parts.append(jnp.dot(sq[:, c0:c0 + w], bd[:w, :w], preferred_element_type=F32))
    ss = jnp.concatenate(parts, axis=1)
    qk = qk * lax.rsqrt(ss * (1.0 / HEAD_DIM) + EPS) * gain_ref[...]

    if use_rope:
        cos = cos_ref[...]
        sin = sin_ref[...]
        lane = lax.broadcasted_iota(jnp.int32, (tm, LANES), 1)
        first = (lane % HEAD_DIM) < (HEAD_DIM // 2)
    outs = []
    for g in range(qkw // LANES):
        blk = qk[:, g * LANES:(g + 1) * LANES]
        if use_rope:
            half = HEAD_DIM // 2
            swapped = jnp.where(first, pltpu.roll(blk, LANES - half, 1), pltpu.roll(blk, half, 1))
            blk = blk * cos + swapped * sin
        outs.append(blk)
    q_ref[0] = jnp.concatenate(outs[:aw // LANES], axis=1).astype(BF16)
    k_ref[0] = outs[aw // LANES].astype(BF16)

    v = res[:, qkw:qkw + kw]
    vt_ref[0] = v.T.astype(BF16)

    cw = (res.shape[1] - qkw - kw) // 2
    a = res[:, qkw + kw:qkw + kw + cw]
    gt = res[:, qkw + kw + cw:]
    y_ref[0] = (a / (1.0 + jnp.exp(-gt))).astype(BF16)


def _inproj(x, sh, sc, ng, w, gain, bd, cos, sin, *, tm):
    b, n, d = x.shape
    in_w = w.shape[1]
    aw = N_HEADS * HEAD_DIM
    kw = N_KV_HEADS * HEAD_DIM
    cw = (in_w - aw - 2 * kw) // 2
    use_rope = cos is not None
    bm = sh.shape[0]
    mod_map = (lambda bi, i: (bi, 0, 0)) if bm > 1 else (lambda bi, i: (0, 0, 0))
    const2 = lambda bi, i: (0, 0)
    in_specs = [
        pl.BlockSpec((1, tm, d), lambda bi, i: (bi, i, 0)),
        pl.BlockSpec((1, 1, d), mod_map),
        pl.BlockSpec((1, 1, d), mod_map),
        pl.BlockSpec((1, d), const2),
        pl.BlockSpec((d, in_w), const2),
        pl.BlockSpec((1, aw + kw), const2),
        pl.BlockSpec((2 * LANES, 2 * LANES), const2),
    ]
    args = [x, sh, sc, ng, w, gain, bd]
    if use_rope:
        in_specs += [pl.BlockSpec((tm, LANES), lambda bi, i: (i, 0))] * 2
        args += [cos, sin]
    out_shape = (
        jax.ShapeDtypeStruct((b, n, aw), BF16),
        jax.ShapeDtypeStruct((b, n, kw), BF16),
        jax.ShapeDtypeStruct((b, kw, n), BF16),
        jax.ShapeDtypeStruct((b, n, cw), BF16),
    )
    out_specs = (
        pl.BlockSpec((1, tm, aw), lambda bi, i: (bi, i, 0)),
        pl.BlockSpec((1, tm, kw), lambda bi, i: (bi, i, 0)),
        pl.BlockSpec((1, kw, tm), lambda bi, i: (bi, 0, i)),
        pl.BlockSpec((1, tm, cw), lambda bi, i: (bi, i, 0)),
    )
    return pl.pallas_call(
        functools.partial(_inproj_kernel, use_rope=use_rope),
        out_shape=out_shape,
        grid=(b, n // tm),
        in_specs=in_specs,
        out_specs=out_specs,
        compiler_params=_params("parallel", "parallel"),
        name="inproj_lat" if use_rope else "inproj_ctx",
    )(*args)


def _attn_kernel(q_ref, k_ref, vt_ref, o_ref, m_ref, acc_ref, sa_ref, sb_ref, ma_ref, mb_ref):
    qblk = q_ref[0]
    n_keys = k_ref.shape[1]
    ck_max = sa_ref.shape[1]
    chunks = [(c0, min(ck_max, n_keys - c0)) for c0 in range(0, n_keys, ck_max)]
    lane = lax.broadcasted_iota(jnp.int32, qblk.shape, 1)
    qms = [jnp.where((lane >= hh * HEAD_DIM) & (lane < (hh + 1) * HEAD_DIM), qblk, jnp.zeros_like(qblk))
           for hh in range(N_KV_HEADS)]
    m_ref[...] = jnp.full(m_ref.shape, NEG_BIG, F32)
    acc_ref[...] = jnp.zeros(acc_ref.shape, F32)

    tq = qblk.shape[0]
    strips = [slice(j, min(j + MXU_COLS, tq)) for j in range(0, tq, MXU_COLS)]

    def scores(chunk, s_ref, mc_ref):
        c0, ck = chunk
        kc = k_ref[0, c0:c0 + ck, :]
        for hh in range(N_KV_HEADS):
            for sl in strips:
                s = lax.dot_general(kc, qms[hh][sl, :], (((1,), (1,)), ((), ())),
                                    preferred_element_type=F32)
                s_ref[hh, :ck, sl] = s
                mc_ref[hh, :, sl] = jnp.max(s, axis=0, keepdims=True)

    def consume(chunk, s_ref, mc_ref):
        c0, ck = chunk
        for hh in range(N_KV_HEADS):
            vc = vt_ref[0, hh * HEAD_DIM:(hh + 1) * HEAD_DIM, c0:c0 + ck]
            vaug = jnp.concatenate([vc, jnp.ones((BF16_ROWS, ck), BF16)], axis=0)
            for sl in strips:
                m_old = m_ref[hh, :, sl]
                m_new = jnp.maximum(m_old, mc_ref[hh, :, sl])
                alpha = jnp.exp2(m_old - m_new)
                p = jnp.exp2(s_ref[hh, :ck, sl] - m_new).astype(BF16)
                acc_ref[hh, :, sl] = (acc_ref[hh, :, sl] * alpha
                                      + jnp.dot(vaug, p, preferred_element_type=F32))
                m_ref[hh, :, sl] = m_new

    bufs = ((sa_ref, ma_ref), (sb_ref, mb_ref))
    scores(chunks[0], *bufs[0])
    for c, chunk in enumerate(chunks):
        if c + 1 < len(chunks):
            scores(chunks[c + 1], *bufs[(c + 1) % 2])
        consume(chunk, *bufs[c % 2])

    for hh in range(N_KV_HEADS):
        acc = acc_ref[hh]
        o = acc[:HEAD_DIM] / acc[HEAD_DIM:HEAD_DIM + 1]
        o_ref[0, hh * HEAD_DIM:(hh + 1) * HEAD_DIM, :] = o.astype(BF16)


def _attention(q, k, vt, *, tq, ck):
    b, n, aw = q.shape
    groups = aw // LANES
    _, n_keys, kw = k.shape
    ck = min(ck, n_keys)
    return pl.pallas_call(
        _attn_kernel,
        out_shape=jax.ShapeDtypeStruct((b, aw, n), BF16),
        grid=(b, groups, n // tq),
        in_specs=[
            pl.BlockSpec((1, tq, LANES), lambda bi, j, qi: (bi, qi, j)),
            pl.BlockSpec((1, n_keys, kw), lambda bi, j, qi: (bi, 0, 0)),
            pl.BlockSpec((1, kw, n_keys), lambda bi, j, qi: (bi, 0, 0)),
        ],
        out_specs=pl.BlockSpec((1, LANES, tq), lambda bi, j, qi: (bi, j, qi)),
        scratch_shapes=[pltpu.VMEM((N_KV_HEADS, 1, tq), F32),
                        pltpu.VMEM((N_KV_HEADS, HEAD_DIM + BF16_ROWS, tq), F32),
                        pltpu.VMEM((N_KV_HEADS, ck, tq), F32),
                        pltpu.VMEM((N_KV_HEADS, ck, tq), F32),
                        pltpu.VMEM((N_KV_HEADS, 1, tq), F32),
                        pltpu.VMEM((N_KV_HEADS, 1, tq), F32)],
        compiler_params=_params("parallel", "parallel", "parallel"),
        name="attention_lat" if n_keys > n else "attention_ctx",
    )(q, k, vt)


CONV_HALO = 16
CONV_ROWS = 32


def _conv_kernel(y_ref, yp_ref, yn_ref, w_ref, b_ref, g_ref, be_ref, o_ref, buf_ref, sh_ref):
    i = pl.program_id(1)
    last = pl.num_programs(1) - 1
    tm = y_ref.shape[1]
    prev = yp_ref[0].astype(F32)
    nxt = yn_ref[0].astype(F32)
    buf_ref[0:CONV_HALO, :] = jnp.where(i > 0, prev, jnp.zeros_like(prev))
    buf_ref[CONV_HALO:CONV_HALO + tm, :] = y_ref[0].astype(F32)
    buf_ref[CONV_HALO + tm:, :] = jnp.where(i < last, nxt, jnp.zeros_like(nxt))
    span = sh_ref.shape[1]
    for res in range(1, SUBLANES):
        sh_ref[res - 1] = buf_ref[res:res + span, :]
    w = w_ref[...]
    off = CONV_HALO - CONV_K // 2
    for r in range(tm // CONV_ROWS):
        r0 = r * CONV_ROWS
        acc = jnp.zeros((CONV_ROWS, w.shape[1]), F32)
        for k in range(CONV_K):
            res = (k + off) % SUBLANES
            base = r0 + k + off - res
            if res == 0:
                tap = buf_ref[base:base + CONV_ROWS, :]
            else:
                tap = sh_ref[res - 1, base:base + CONV_ROWS, :]
            acc = acc + w[k:k + 1, :] * tap
        acc = acc + b_ref[...]
        mu = jnp.mean(acc, axis=-1, keepdims=True)
        cen = acc - mu
        var = jnp.mean(cen * cen, axis=-1, keepdims=True)
        z = cen * lax.rsqrt(var + EPS) * g_ref[...] + be_ref[...]
        o_ref[0, r0:r0 + CONV_ROWS, :] = _silu(z).astype(BF16)


def _conformer_conv(y, w_dw, b_dw, ln_g, ln_b, *, tm):
    b, n, cw = y.shape
    hb = tm // CONV_HALO
    nh = n // CONV_HALO
    vec = pl.BlockSpec((1, cw), lambda bi, i: (0, 0))
    return pl.pallas_call(
        _conv_kernel,
        out_shape=jax.ShapeDtypeStruct((b, n, cw), BF16),
        grid=(b, n // tm),
        in_specs=[
            pl.BlockSpec((1, tm, cw), lambda bi, i: (bi, i, 0)),
            pl.BlockSpec((1, CONV_HALO, cw), lambda bi, i: (bi, jnp.maximum(i * hb - 1, 0), 0)),
            pl.BlockSpec((1, CONV_HALO, cw), lambda bi, i: (bi, jnp.minimum((i + 1) * hb, nh - 1), 0)),
            pl.BlockSpec((CONV_K, cw), lambda bi, i: (0, 0)),
            vec, vec, vec,
        ],
        out_specs=pl.BlockSpec((1, tm, cw), lambda bi, i: (bi, i, 0)),
        scratch_shapes=[pltpu.VMEM((tm + 2 * CONV_HALO, cw), F32),
                        pltpu.VMEM((SUBLANES - 1, tm + 2 * CONV_HALO - SUBLANES, cw), F32)],
        compiler_params=_params("parallel", "parallel"),
        name="conformer_conv",
    )(y, y, y, w_dw, b_dw.reshape(1, cw), ln_g.reshape(1, cw), ln_b.reshape(1, cw))


def _outproj_kernel(at_ref, c_ref, wa_ref, wc_ref, x_ref, g_ref, o_ref):
    o = lax.dot_general(at_ref[0], wa_ref[...], (((0,), (0,)), ((), ())),
                        preferred_element_type=F32)
    o = o + jnp.dot(c_ref[0], wc_ref[...], preferred_element_type=F32)
    o_ref[0] = x_ref[0] + g_ref[0] * o


def _outproj(a_t, c, w_a, w_c, x, gate, *, tm):
    b, n, d = x.shape
    aw = a_t.shape[1]
    cw = c.shape[2]
    bm = gate.shape[0]
    mod_map = (lambda bi, i: (bi, 0, 0)) if bm > 1 else (lambda bi, i: (0, 0, 0))
    return pl.pallas_call(
        _outproj_kernel,
        out_shape=jax.ShapeDtypeStruct((b, n, d), F32),
        grid=(b, n // tm),
        in_specs=[
            pl.BlockSpec((1, aw, tm), lambda bi, i: (bi, 0, i)),
            pl.BlockSpec((1, tm, cw), lambda bi, i: (bi, i, 0)),
            pl.BlockSpec((aw, d), lambda bi, i: (0, 0)),
            pl.BlockSpec((cw, d), lambda bi, i: (0, 0)),
            pl.BlockSpec((1, tm, d), lambda bi, i: (bi, i, 0)),
            pl.BlockSpec((1, 1, d), mod_map),
        ],
        out_specs=pl.BlockSpec((1, tm, d), lambda bi, i: (bi, i, 0)),
        compiler_params=_params("parallel", "parallel"),
        name="outproj",
    )(a_t, c, w_a, w_c, x, gate)


def _swiglu(h, w1_ref, w3_ref, w2_ref, ff_chunk):
    d_ff = w1_ref.shape[2]
    f = None
    for c0 in range(0, d_ff, ff_chunk):
        a = jnp.dot(h, w1_ref[0, :, c0:c0 + ff_chunk], preferred_element_type=F32)
        b = jnp.dot(h, w3_ref[0, :, c0:c0 + ff_chunk], preferred_element_type=F32)
        z = (_silu(a) * b).astype(BF16)
        part = jnp.dot(z, w2_ref[0, c0:c0 + ff_chunk, :], preferred_element_type=F32)
        f = part if f is None else f + part
    return f


def _ffn_kernel(x_ref, sh_ref, sc_ref, g_ref, ng_ref, w1_ref, w3_ref, w2_ref, o_ref, *, ff_chunk):
    h = _norm_mod(x_ref[0], ng_ref[...], sc_ref[0], sh_ref[0]).astype(BF16)
    o_ref[0] = x_ref[0] + g_ref[0] * _swiglu(h, w1_ref, w3_ref, w2_ref, ff_chunk)


def _ffn(x, sh, sc, gate, ng, w1, w3, w2, *, tm):
    b, n, d = x.shape
    d_ff = w1.shape[2]
    bm = sh.shape[0]
    mod_map = (lambda bi, i: (bi, 0, 0)) if bm > 1 else (lambda bi, i: (0, 0, 0))
    mod = pl.BlockSpec((1, 1, d), mod_map)
    return pl.pallas_call(
        functools.partial(_ffn_kernel, ff_chunk=d_ff // 2),
        out_shape=jax.ShapeDtypeStruct((b, n, d), F32),
        grid=(b, n // tm),
        in_specs=[
            pl.BlockSpec((1, tm, d), lambda bi, i: (bi, i, 0)),
            mod, mod, mod,
            pl.BlockSpec((1, d), lambda bi, i: (0, 0)),
            pl.BlockSpec((1, d, d_ff), lambda bi, i: (0, 0, 0)),
            pl.BlockSpec((1, d, d_ff), lambda bi, i: (0, 0, 0)),
            pl.BlockSpec((1, d_ff, d), lambda bi, i: (0, 0, 0)),
        ],
        out_specs=pl.BlockSpec((1, tm, d), lambda bi, i: (bi, i, 0)),
        compiler_params=_params("parallel", "parallel"),
        name="dense_ffn",
    )(x, sh, sc, gate, ng, w1, w3, w2)


META_G1, META_G2, META_E1, META_E2, META_R1, META_R2 = range(6)
TOK_ROWS = 8


def _to_token_tiles(ref, val, lead=()):
    n = val.shape[0]
    for c in range(TOK_ROWS):
        ref[lead + (pl.ds(c, n, stride=TOK_ROWS), slice(None))] = val[:, c * LANES:(c + 1) * LANES]


def _from_token_tiles(ref, n, lead=()):
    return jnp.concatenate([ref[lead + (pl.ds(c, n, stride=TOK_ROWS), slice(None))]
                            for c in range(TOK_ROWS)], axis=1)


def _router_kernel(x_ref, sh_ref, sc_ref, ng_ref, wr_ref, br_ref, tri_ref,
                   h_ref, meta_ref, cnt_ref, carry_ref):
    i = pl.program_id(0)

    @pl.when(i == 0)
    def _():
        carry_ref[...] = jnp.zeros(carry_ref.shape, F32)

    h = _norm_mod(x_ref[...], ng_ref[...], sc_ref[0], sh_ref[0])
    _to_token_tiles(h_ref, h)
    w = wr_ref[...]
    w_hi = w.astype(BF16)
    w_lo = (w - w_hi.astype(F32)).astype(BF16)
    h_hi = h.astype(BF16)
    h_lo = (h - h_hi.astype(F32)).astype(BF16)
    logits = (jnp.dot(h_hi, w_hi, preferred_element_type=F32)
              + (jnp.dot(h_lo, w_hi, preferred_element_type=F32)
                 + jnp.dot(h_hi, w_lo, preferred_element_type=F32))) + br_ref[...]
    lane = lax.broadcasted_iota(jnp.int32, logits.shape, 1)
    t1 = jnp.max(logits, axis=-1, keepdims=True)
    i1 = jnp.min(jnp.where(logits == t1, lane, LANES), axis=-1, keepdims=True)
    rest = jnp.where(lane == i1, NEG_BIG, logits)
    t2 = jnp.max(rest, axis=-1, keepdims=True)
    i2 = jnp.min(jnp.where(rest == t2, lane, LANES), axis=-1, keepdims=True)
    e2 = jnp.exp(t2 - t1)
    den = 1.0 + e2
    onehot = jnp.where((lane == i1) | (lane == i2), 1.0, 0.0)
    before = carry_ref[...] + jnp.dot(tri_ref[...], onehot.astype(BF16), preferred_element_type=F32)
    r1 = jnp.sum(jnp.where(lane == i1, before, 0.0), axis=-1, keepdims=True)
    r2 = jnp.sum(jnp.where(lane == i2, before, 0.0), axis=-1, keepdims=True)
    meta = jnp.zeros(logits.shape, F32)
    for col, val in ((META_G1, 1.0 / den), (META_G2, e2 / den), (META_E1, i1.astype(F32)),
                     (META_E2, i2.astype(F32)), (META_R1, r1), (META_R2, r2)):
        meta = jnp.where(lane == col, val, meta)
    meta_ref[...] = meta
    carry_ref[...] = carry_ref[...] + jnp.sum(onehot, axis=0, keepdims=True)
    cnt_ref[...] = carry_ref[...]


def _router(x2, sh, sc, ng, w_r, b_r, tri, *, tm, rows_per_mod):
    n, d = x2.shape
    per = rows_per_mod // tm
    mod_map = lambda i: (i // per, 0, 0)
    const = lambda i: (0, 0)
    return pl.pallas_call(
        _router_kernel,
        out_shape=(jax.ShapeDtypeStruct((n * TOK_ROWS, LANES), F32),
                   jax.ShapeDtypeStruct((n, LANES), F32),
                   jax.ShapeDtypeStruct((1, LANES), F32)),
        grid=(n // tm,),
        in_specs=[
            pl.BlockSpec((tm, d), lambda i: (i, 0)),
            pl.BlockSpec((1, 1, d), mod_map),
            pl.BlockSpec((1, 1, d), mod_map),
            pl.BlockSpec((1, d), const),
            pl.BlockSpec((d, LANES), const),
            pl.BlockSpec((1, LANES), const),
            pl.BlockSpec((tm, tm), const),
        ],
        out_specs=(pl.BlockSpec((tm * TOK_ROWS, LANES), lambda i: (i, 0)),
                   pl.BlockSpec((tm, LANES), lambda i: (i, 0)),
                   pl.BlockSpec((1, LANES), const)),
        scratch_shapes=[pltpu.VMEM((1, LANES), F32)],
        compiler_params=_params("arbitrary"),
        name="moe_router",
    )(x2, sh, sc, ng, w_r, b_r, tri)


def _tok(ref, t, count=1):
    start = t * TOK_ROWS
    if not isinstance(t, int):
        start = pl.multiple_of(start, TOK_ROWS)
    return ref.at[pl.ds(start, count * TOK_ROWS)]


def _row_copy(src_ref, src_row, dst_ref, dst_row, sem):
    return pltpu.make_async_copy(_tok(src_ref, src_row), _tok(dst_ref, dst_row), sem)


ROW_UNROLL = 8


def _scatter_kernel(tail_ref, pos_ref, h_ref, hs_ref, zero_ref, sem):
    i = pl.program_id(0)
    tm = h_ref.shape[0] // TOK_ROWS
    tile = zero_ref.shape[0] // TOK_ROWS

    @pl.when(i == 0)
    def _():
        zero_ref[...] = jnp.zeros(zero_ref.shape, F32)
        n_tiles = hs_ref.shape[0] // zero_ref.shape[0]
        n_used = tail_ref[N_EXPERTS]

        def zero_tile(first_row):
            return pltpu.make_async_copy(zero_ref, _tok(hs_ref, first_row, tile), sem)

        for e in range(N_EXPERTS):
            zero_tile(tail_ref[e]).start()
        lax.fori_loop(n_used, n_tiles, lambda t, c: (zero_tile(t * tile).start(), c)[1], 0)
        for e in range(N_EXPERTS):
            zero_tile(tail_ref[e]).wait()
        lax.fori_loop(n_used, n_tiles, lambda t, c: (zero_tile(t * tile).wait(), c)[1], 0)

    def issue(r, carry):
        for k in range(2):
            _row_copy(h_ref, r, hs_ref, pos_ref[0, 0, k * tm + r], sem).start(priority=k)
        return carry

    lax.fori_loop(0, tm, issue, 0, unroll=ROW_UNROLL)

    def drain(r, carry):
        _row_copy(h_ref, 0, hs_ref, 0, sem).wait()
        return carry

    lax.fori_loop(0, 2 * tm, drain, 0, unroll=ROW_UNROLL)


def _scatter(tails, pos, h, *, tm, n_rows, tile):
    n = h.shape[0] // TOK_ROWS
    return pl.pallas_call(
        _scatter_kernel,
        out_shape=jax.ShapeDtypeStruct((n_rows * TOK_ROWS, LANES), F32),
        grid_spec=pltpu.PrefetchScalarGridSpec(
            num_scalar_prefetch=1,
            grid=(n // tm,),
            in_specs=[
                pl.BlockSpec((1, 1, 2 * tm), lambda i, tails: (i, 0, 0), memory_space=pltpu.SMEM),
                pl.BlockSpec((tm * TOK_ROWS, LANES), lambda i, tails: (i, 0)),
            ],
            out_specs=pl.BlockSpec(memory_space=pl.ANY),
            scratch_shapes=[pltpu.VMEM((tile * TOK_ROWS, LANES), F32), pltpu.SemaphoreType.DMA],
        ),
        compiler_params=_params("arbitrary"),
        name="moe_scatter",
    )(tails, pos, h)


def _expert_kernel(te_ref, nu_ref, hs_ref, w1_ref, w3_ref, w2_ref, y_ref, *, ff_chunk):
    used = pl.program_id(0) < nu_ref[0]

    @pl.when(used)
    def _():
        tile = hs_ref.shape[0] // TOK_ROWS
        h = _from_token_tiles(hs_ref, tile).astype(BF16)
        _to_token_tiles(y_ref, _swiglu(h, w1_ref, w3_ref, w2_ref, ff_chunk))

    @pl.when(jnp.logical_not(used))
    def _():
        y_ref[...] = jnp.zeros(y_ref.shape, F32)


def _experts(tile_e, n_used, hs, w1, w3, w2, *, tile):
    n_rows = hs.shape[0] // TOK_ROWS
    d = w1.shape[1]
    d_ff = w1.shape[2]
    row_map = lambda t, te, nu: (t, 0)
    w_map = lambda t, te, nu: (te[t], 0, 0)
    return pl.pallas_call(
        functools.partial(_expert_kernel, ff_chunk=d_ff // 2),
        out_shape=jax.ShapeDtypeStruct((n_rows * TOK_ROWS, LANES), F32),
        grid_spec=pltpu.PrefetchScalarGridSpec(
            num_scalar_prefetch=2,
            grid=(n_rows // tile,),
            in_specs=[
                pl.BlockSpec((tile * TOK_ROWS, LANES), row_map),
                pl.BlockSpec((1, d, d_ff), w_map),
                pl.BlockSpec((1, d, d_ff), w_map),
                pl.BlockSpec((1, d_ff, d), w_map),
            ],
            out_specs=pl.BlockSpec((tile * TOK_ROWS, LANES), row_map),
        ),
        compiler_params=_params("arbitrary"),
        name="moe_experts",
    )(tile_e, n_used, hs, w1, w3, w2)


def _combine_kernel(pos_ref, pos_next_ref, x_ref, g_ref, meta_ref, y_ref, *rest, final_norm):
    fg_ref = rest[0] if final_norm else None
    o_ref, buf_ref, sems = rest[-3:]
    tm = x_ref.shape[0]
    i = pl.program_id(0)
    slot = i % 2

    def gather(p_ref, dst):
        def issue(r, carry):
            for k in range(2):
                _row_copy(y_ref, p_ref[0, 0, k * tm + r], buf_ref.at[dst, k], r,
                          sems.at[dst]).start(priority=k)
            return carry
        lax.fori_loop(0, tm, issue, 0, unroll=ROW_UNROLL)

    @pl.when(i == 0)
    def _():
        gather(pos_ref, 0)

    @pl.when(i + 1 < pl.num_programs(0))
    def _():
        gather(pos_next_ref, 1 - slot)

    def drain(r, carry):
        _row_copy(y_ref, 0, buf_ref.at[slot, 0], 0, sems.at[slot]).wait()
        return carry

    lax.fori_loop(0, 2 * tm, drain, 0, unroll=ROW_UNROLL)
    meta = meta_ref[...]
    g1 = meta[:, META_G1:META_G1 + 1]
    g2 = meta[:, META_G2:META_G2 + 1]
    y1 = _from_token_tiles(buf_ref, tm, (slot, 0))
    y2 = _from_token_tiles(buf_ref, tm, (slot, 1))
    out = x_ref[...] + g_ref[0] * (g1 * y1 + g2 * y2)
    if final_norm:
        ms = jnp.mean(out * out, axis=-1, keepdims=True)
        out = out * lax.rsqrt(ms + EPS) * fg_ref[...]
    o_ref[...] = out


def _combine(pos, x2, gate, meta, y, final_g=None, *, tm, rows_per_mod):
    n, d = x2.shape
    per = rows_per_mod // tm
    last = n // tm - 1
    in_specs = [
        pl.BlockSpec((1, 1, 2 * tm), lambda i: (i, 0, 0), memory_space=pltpu.SMEM),
        pl.BlockSpec((1, 1, 2 * tm), lambda i: (jnp.minimum(i + 1, last), 0, 0), memory_space=pltpu.SMEM),
        pl.BlockSpec((tm, d), lambda i: (i, 0)),
        pl.BlockSpec((1, 1, d), lambda i: (i // per, 0, 0)),
        pl.BlockSpec((tm, LANES), lambda i: (i, 0)),
        pl.BlockSpec(memory_space=pl.ANY),
    ]
    args = [pos, pos, x2, gate, meta, y]
    if final_g is not None:
        in_specs.append(pl.BlockSpec((1, d), lambda i: (0, 0)))
        args.append(final_g.reshape(1, d))
    return pl.pallas_call(
        functools.partial(_combine_kernel, final_norm=final_g is not None),
        out_shape=jax.ShapeDtypeStruct((n, d), F32),
        grid=(n // tm,),
        in_specs=in_specs,
        out_specs=pl.BlockSpec((tm, d), lambda i: (i, 0)),
        scratch_shapes=[pltpu.VMEM((2, 2, tm * TOK_ROWS, LANES), F32), pltpu.SemaphoreType.DMA((2,))],
        compiler_params=_params("arbitrary"),
        name="moe_combine",
    )(*args)


def _moe(x, sh, sc, gate, ng, w1, w3, w2, w_r, b_r, tri, final_g=None, *, tm, tile):
    b, n, d = x.shape
    rows = b * n
    rows_per_mod = n if sh.shape[0] > 1 else rows
    x2 = x.reshape(rows, d)
    h, meta, cnt = _router(x2, sh, sc, ng, w_r, b_r, tri, tm=tm, rows_per_mod=rows_per_mod)

    counts = cnt[0, :N_EXPERTS].astype(jnp.int32)
    padded = (counts + tile - 1) // tile * tile
    ends = jnp.cumsum(padded)
    starts = ends - padded
    experts = meta[:, META_E1:META_E2 + 1].astype(jnp.int32)
    ranks = meta[:, META_R1:META_R2 + 1].astype(jnp.int32)
    pos = starts[experts] + ranks
    pos = pos.reshape(rows // tm, tm, 2).transpose(0, 2, 1).reshape(rows // tm, 1, 2 * tm)
    n_rows = 2 * rows + N_EXPERTS * tile
    n_tiles = n_rows // tile
    n_used = (ends[-1] // tile).astype(jnp.int32)
    t = jnp.minimum(jnp.arange(n_tiles, dtype=jnp.int32), n_used - 1)
    tile_e = jnp.minimum(jnp.searchsorted(ends, t * tile, side="right"), N_EXPERTS - 1).astype(jnp.int32)
    busiest = jnp.argmax(counts)
    tails = jnp.where(padded > 0, ends - tile, ends[busiest] - tile).astype(jnp.int32)
    tails = jnp.concatenate([tails, n_used.reshape(1)])

    hs = _scatter(tails, pos, h, tm=tm, n_rows=n_rows, tile=tile)
    y = _experts(tile_e, n_used.reshape(1), hs, w1, w3, w2, tile=tile)
    out = _combine(pos, x2, gate, meta, y, final_g, tm=tm, rows_per_mod=rows_per_mod)
    return out.reshape(b, n, d)


def _final_kernel(x_ref, g_ref, o_ref):
    x = x_ref[0]
    ms = jnp.mean(x * x, axis=-1, keepdims=True)
    o_ref[0] = x * lax.rsqrt(ms + EPS) * g_ref[...]


def _final_norm(x, g, *, tm):
    b, n, d = x.shape
    return pl.pallas_call(
        _final_kernel,
        out_shape=jax.ShapeDtypeStruct((b, n, d), F32),
        grid=(b, n // tm),
        in_specs=[pl.BlockSpec((1, tm, d), lambda bi, i: (bi, i, 0)),
                  pl.BlockSpec((1, d), lambda bi, i: (0, 0))],
        out_specs=pl.BlockSpec((1, tm, d), lambda bi, i: (bi, i, 0)),
        compiler_params=_params("parallel", "parallel"),
        name="final_norm",
    )(x, g.reshape(1, d))


def _qk_column_order():
    half = np.concatenate([np.arange(0, HEAD_DIM, 2), np.arange(1, HEAD_DIM, 2)])
    per_group = N_HEADS // N_KV_HEADS
    heads = [h for j in range(per_group) for h in range(j, N_HEADS, per_group)]
    q_cols = np.concatenate([h * HEAD_DIM + half for h in heads])
    k_cols = N_HEADS * HEAD_DIM + np.concatenate([g * HEAD_DIM + half for g in range(N_KV_HEADS)])
    head_rows = np.concatenate([h * HEAD_DIM + np.arange(HEAD_DIM) for h in heads])
    return half, q_cols, k_cols, head_rows


def _rope_tables(n_tokens):
    rows = n_tokens // GRID_W
    row = jnp.repeat(jnp.arange(rows), GRID_W).astype(F32)
    col = jnp.tile(jnp.arange(GRID_W), rows).astype(F32)
    half = HEAD_DIM // 2
    inv = ROPE_THETA ** (-jnp.arange(0, half, 2, dtype=F32) / half)
    ang = jnp.concatenate([row[:, None] * inv, col[:, None] * inv], axis=-1)
    cos, sin = jnp.cos(ang), jnp.sin(ang)
    reps = LANES // HEAD_DIM
    return (jnp.tile(jnp.concatenate([cos, cos], axis=-1), (1, reps)),
            jnp.tile(jnp.concatenate([-sin, sin], axis=-1), (1, reps)))


ATTN_CHUNK = 512
MOE_ROWS = 512
MOE_TILE = 512


def _tri(n):
    return jnp.asarray(np.tril(np.ones((n, n), np.float32), -1), BF16)


def _pick(n, pref):
    t = min(n, pref)
    while n % t:
        t //= 2
    return t


def kernel(x, c, ctx, c_ctx, w_ada, b_ada, norm1_g, w_in, q_norm_g, k_norm_g, dw_w, dw_b,
           conv_ln_g, conv_ln_b, w_out, norm2_g, ffn_w1, ffn_w3, ffn_w2, router_w, router_b,
           exp_w1, exp_w3, exp_w2, final_g):
    bsz, s, d = x.shape
    n_ctx = ctx.shape[1]
    depth = w_ada.shape[0]
    aw = N_HEADS * HEAD_DIM
    kw = N_KV_HEADS * HEAD_DIM

    half, q_cols, k_cols, head_rows = _qk_column_order()
    cols = np.concatenate([q_cols, k_cols, np.arange(aw + kw, w_in.shape[2])])
    w_in_p = w_in[:, :, cols].astype(BF16)
    gain = jnp.concatenate([jnp.tile(q_norm_g[:, half], (1, N_HEADS)) * (HEAD_DIM ** -0.5 * np.log2(np.e)),
                            jnp.tile(k_norm_g[:, half], (1, N_KV_HEADS))], axis=1)
    w_out_a = w_out[:, head_rows, :].astype(BF16)
    w_out_c = w_out[:, aw:, :].astype(BF16)
    blk = np.kron(np.eye(2 * LANES // HEAD_DIM), np.ones((HEAD_DIM, HEAD_DIM)))
    bd = jnp.asarray(blk, BF16)
    cos, sin = _rope_tables(s)

    pad = (-(bsz + 1)) % 8
    cvec = jnp.concatenate([c, c_ctx[None, :], jnp.zeros((pad, d), F32)], axis=0)
    mods = _modulation(cvec, w_ada, b_ada)

    tm_lat = _pick(s, 1024)
    tq_lat = _pick(s, 512)
    tm_ctx = n_ctx
    tm_conv = _pick(s, 256)
    tm_ffn = _pick(s, 512)

    xc = ctx
    for l in range(depth):
        last = l == depth - 1
        m = [mods[l, :, i * d:(i + 1) * d] for i in range(6)]
        lat = [v[:bsz, None, :] for v in m]
        cx = [v[bsz:bsz + 1, None, :] for v in m]
        ng1 = norm1_g[l].reshape(1, d)
        ng2 = norm2_g[l].reshape(1, d)
        gl = gain[l].reshape(1, aw + kw)

        q, k, vt, y = _inproj(x, lat[0], lat[1], ng1, w_in_p[l], gl, bd, cos, sin, tm=tm_lat)
        qc, kc, vtc, yc = _inproj(xc, cx[0], cx[1], ng1, w_in_p[l], gl, bd, None, None,
                                  tm=tm_ctx)
        k_all = jnp.concatenate([k, kc], axis=1)
        vt_all = jnp.concatenate([vt, vtc], axis=2)
        a_t = _attention(q, k_all, vt_all, tq=tq_lat, ck=ATTN_CHUNK)
        c_lat = _conformer_conv(y, dw_w[l], dw_b[l], conv_ln_g[l], conv_ln_b[l], tm=tm_conv)
        x = _outproj(a_t, c_lat, w_out_a[l], w_out_c[l], x, lat[2], tm=tm_ffn)
        if not last:
            a_tc = _attention(qc, kc, vtc, tq=n_ctx, ck=ATTN_CHUNK)
            c_c = _conformer_conv(yc, dw_w[l], dw_b[l], conv_ln_g[l], conv_ln_b[l], tm=n_ctx)
            xc = _outproj(a_tc, c_c, w_out_a[l], w_out_c[l], xc, cx[2], tm=n_ctx)

        i = l // 2
        if l % 2 == 0:
            w1 = ffn_w1[i:i + 1].astype(BF16)
            w3 = ffn_w3[i:i + 1].astype(BF16)
            w2 = ffn_w2[i:i + 1].astype(BF16)
            x = _ffn(x, lat[3], lat[4], lat[5], ng2, w1, w3, w2, tm=tm_ffn)
            if not last:
                xc = _ffn(xc, cx[3], cx[4], cx[5], ng2, w1, w3, w2, tm=n_ctx)
        else:
            w1 = exp_w1[i].astype(BF16)
            w3 = exp_w3[i].astype(BF16)
            w2 = exp_w2[i].astype(BF16)
            w_r = jnp.zeros((d, LANES), F32).at[:, :N_EXPERTS].set(router_w[i])
            b_r = jnp.full((1, LANES), NEG_BIG, F32).at[0, :N_EXPERTS].set(router_b[i])
            tm_l = _pick(s, MOE_ROWS)
            x = _moe(x, lat[3], lat[4], lat[5], ng2, w1, w3, w2, w_r, b_r, _tri(tm_l),
                     final_g if last else None, tm=tm_l, tile=MOE_TILE)
            if not last:
                tm_c = _pick(bsz * n_ctx, MOE_ROWS)
                xc = _moe(xc, cx[3], cx[4], cx[5], ng2, w1, w3, w2, w_r, b_r, _tri(tm_c),
                          tm=tm_c, tile=MOE_TILE)
    if depth % 2 == 0:
        return x
    return _final_norm(x, final_g, tm=tm_ffn)
```

```python
import functools

import numpy as np
import jax
import jax.numpy as jnp
from jax import lax
from jax.experimental import pallas as pl
from jax.experimental.pallas import tpu as pltpu

F32 = jnp.float32
BF16 = jnp.bfloat16

N_HEADS = 8
N_KV_HEADS = 2
HEAD_DIM = 64
GRID_W = 64
CONV_K = 31
N_EXPERTS = 8
ROPE_THETA = 10000.0
EPS = 1e-6

LANES = 128
SUBLANES = 8
MXU_COLS = 256
BF16_ROWS = 16
VMEM_LIMIT = 56 * 1024 * 1024
NEG_BIG = -1e30


def _params(*sem):
    return pltpu.CompilerParams(dimension_semantics=sem, vmem_limit_bytes=VMEM_LIMIT)


def _silu(v):
    return v / (1.0 + jnp.exp(-v))


def _mod_kernel(c_ref, w_ref, b_ref, o_ref):
    c = c_ref[...]
    o_ref[0] = jnp.dot(_silu(c), w_ref[0], preferred_element_type=F32,
                       precision=lax.Precision.HIGHEST) + b_ref[0]


def _modulation(cvec, w_ada, b_ada):
    depth, d, n = w_ada.shape
    r = cvec.shape[0]
    tn = 1536
    return pl.pallas_call(
        _mod_kernel,
        out_shape=jax.ShapeDtypeStruct((depth, r, n), F32),
        grid=(depth, n // tn),
        in_specs=[
            pl.BlockSpec((r, d), lambda l, j: (0, 0)),
            pl.BlockSpec((1, d, tn), lambda l, j: (l, 0, j)),
            pl.BlockSpec((1, 1, tn), lambda l, j: (l, 0, j)),
        ],
        out_specs=pl.BlockSpec((1, r, tn), lambda l, j: (l, 0, j)),
        compiler_params=_params("parallel", "parallel"),
        name="modulation",
    )(cvec, w_ada, b_ada.reshape(depth, 1, n))


def _norm_mod(x, ng, sc, sh):
    ms = jnp.mean(x * x, axis=-1, keepdims=True)
    return (x * lax.rsqrt(ms + EPS)) * ng * (1.0 + sc) + sh


def _inproj_kernel(*refs, use_rope):
    if use_rope:
        (x_ref, sh_ref, sc_ref, ng_ref, w_ref, gain_ref, bd_ref, cos_ref, sin_ref,
         q_ref, k_ref, vt_ref, y_ref) = refs
    else:
        (x_ref, sh_ref, sc_ref, ng_ref, w_ref, gain_ref, bd_ref,
         q_ref, k_ref, vt_ref, y_ref) = refs
    aw = N_HEADS * HEAD_DIM
    kw = N_KV_HEADS * HEAD_DIM
    qkw = aw + kw
    x = x_ref[0]
    tm = x.shape[0]
    h = _norm_mod(x, ng_ref[...], sc_ref[0], sh_ref[0]).astype(BF16)
    res = jnp.dot(h, w_ref[...], preferred_element_type=F32)

    qk = res[:, :qkw]
    sq = (qk * qk).astype(BF16)
    bd = bd_ref[...]
    parts = []
    for c0 in range(0, qkw, 2 * LANES):
        w = min(2 * LANES, qkw - c0)
        parts.append(jnp.dot(sq[:, c0:c0 + w], bd[:w, :w], preferred_element_type=F32))
    ss = jnp.concatenate(parts, axis=1)
    qk = qk * lax.rsqrt(ss * (1.0 / HEAD_DIM) + EPS) * gain_ref[...]

    if use_rope:
        cos = cos_ref[...]
        sin = sin_ref[...]
        lane = lax.broadcasted_iota(jnp.int32, (tm, LANES), 1)
        first = (lane % HEAD_DIM) < (HEAD_DIM // 2)
    outs = []
    for g in range(qkw // LANES):
        blk = qk[:, g * LANES:(g + 1) * LANES]
        if use_rope:
            half = HEAD_DIM // 2
            swapped = jnp.where(first, pltpu.roll(blk, LANES - half, 1), pltpu.roll(blk, half, 1))
            blk = blk * cos + swapped * sin
        outs.append(blk)
    q_ref[0] = jnp.concatenate(outs[:aw // LANES], axis=1).astype(BF16)
    k_ref[0] = outs[aw // LANES].astype(BF16)

    v = res[:, qkw:qkw + kw]
    vt_ref[0] = v.T.astype(BF16)

    cw = (res.shape[1] - qkw - kw) // 2
    a = res[:, qkw + kw:qkw + kw + cw]
    gt = res[:, qkw + kw + cw:]
    y_ref[0] = (a / (1.0 + jnp.exp(-gt))).astype(BF16)


def _inproj(x, sh, sc, ng, w, gain, bd, cos, sin, *, tm):
    b, n, d = x.shape
    in_w = w.shape[1]
    aw = N_HEADS * HEAD_DIM
    kw = N_KV_HEADS * HEAD_DIM
    cw = (in_w - aw - 2 * kw) // 2
    use_rope = cos is not None
    bm = sh.shape[0]
    mod_map = (lambda bi, i: (bi, 0, 0)) if bm > 1 else (lambda bi, i: (0, 0, 0))
    const2 = lambda bi, i: (0, 0)
    in_specs = [
        pl.BlockSpec((1, tm, d), lambda bi, i: (bi, i, 0)),
        pl.BlockSpec((1, 1, d), mod_map),
        pl.BlockSpec((1, 1, d), mod_map),
        pl.BlockSpec((1, d), const2),
        pl.BlockSpec((d, in_w), const2),
        pl.BlockSpec((1, aw + kw), const2),
        pl.BlockSpec((2 * LANES, 2 * LANES), const2),
    ]
    args = [x, sh, sc, ng, w, gain, bd]
    if use_rope:
        in_specs += [pl.BlockSpec((tm, LANES), lambda bi, i: (i, 0))] * 2
        args += [cos, sin]
    out_shape = (
        jax.ShapeDtypeStruct((b, n, aw), BF16),
        jax.ShapeDtypeStruct((b, n, kw), BF16),
        jax.ShapeDtypeStruct((b, kw, n), BF16),
        jax.ShapeDtypeStruct((b, n, cw), BF16),
    )
    out_specs = (
        pl.BlockSpec((1, tm, aw), lambda bi, i: (bi, i, 0)),
        pl.BlockSpec((1, tm, kw), lambda bi, i: (bi, i, 0)),
        pl.BlockSpec((1, kw, tm), lambda bi, i: (bi, 0, i)),
        pl.BlockSpec((1, tm, cw), lambda bi, i: (bi, i, 0)),
    )
    return pl.pallas_call(
        functools.partial(_inproj_kernel, use_rope=use_rope),
        out_shape=out_shape,
        grid=(b, n // tm),
        in_specs=in_specs,
        out_specs=out_specs,
        compiler_params=_params("parallel", "parallel"),
        name="inproj_lat" if use_rope else "inproj_ctx",
    )(*args)


def _attn_kernel(q_ref, k_ref, vt_ref, o_ref, m_ref, acc_ref, sa_ref, sb_ref, ma_ref, mb_ref):
    qblk = q_ref[0]
    n_keys = k_ref.shape[1]
    ck_max = sa_ref.shape[1]
    chunks = [(c0, min(ck_max, n_keys - c0)) for c0 in range(0, n_keys, ck_max)]
    lane = lax.broadcasted_iota(jnp.int32, qblk.shape, 1)
    qms = [jnp.where((lane >= hh * HEAD_DIM) & (lane < (hh + 1) * HEAD_DIM), qblk, jnp.zeros_like(qblk))
           for hh in range(N_KV_HEADS)]
    m_ref[...] = jnp.full(m_ref.shape, NEG_BIG, F32)
    acc_ref[...] = jnp.zeros(acc_ref.shape, F32)

    tq = qblk.shape[0]
    strips = [slice(j, min(j + MXU_COLS, tq)) for j in range(0, tq, MXU_COLS)]

    def scores(chunk, s_ref, mc_ref):
        c0, ck = chunk
        kc = k_ref[0, c0:c0 + ck, :]
        for hh in range(N_KV_HEADS):
            for sl in strips:
                s = lax.dot_general(kc, qms[hh][sl, :], (((1,), (1,)), ((), ())),
                                    preferred_element_type=F32)
                s_ref[hh, :ck, sl] = s
                mc_ref[hh, :, sl] = jnp.max(s, axis=0, keepdims=True)

    def consume(chunk, s_ref, mc_ref):
        c0, ck = chunk
        for hh in range(N_KV_HEADS):
            vc = vt_ref[0, hh * HEAD_DIM:(hh + 1) * HEAD_DIM, c0:c0 + ck]
            vaug = jnp.concatenate([vc, jnp.ones((BF16_ROWS, ck), BF16)], axis=0)
            for sl in strips:
                m_old = m_ref[hh, :, sl]
                m_new = jnp.maximum(m_old, mc_ref[hh, :, sl])
                alpha = jnp.exp2(m_old - m_new)
                p = jnp.exp2(s_ref[hh, :ck, sl] - m_new).astype(BF16)
                acc_ref[hh, :, sl] = (acc_ref[hh, :, sl] * alpha
                                      + jnp.dot(vaug, p, preferred_element_type=F32))
                m_ref[hh, :, sl] = m_new

    bufs = ((sa_ref, ma_ref), (sb_ref, mb_ref))
    scores(chunks[0], *bufs[0])
    for c, chunk in enumerate(chunks):
        if c + 1 < len(chunks):
            scores(chunks[c + 1], *bufs[(c + 1) % 2])
        consume(chunk, *bufs[c % 2])

    for hh in range(N_KV_HEADS):
        acc = acc_ref[hh]
        o = acc[:HEAD_DIM] / acc[HEAD_DIM:HEAD_DIM + 1]
        o_ref[0, hh * HEAD_DIM:(hh + 1) * HEAD_DIM, :] = o.astype(BF16)


def _attn_bounded_kernel(ub_ref, q_ref, k_ref, vt_ref, o_ref, *, ck_max):
    qblk = q_ref[0]
    tq = qblk.shape[0]
    n_keys = k_ref.shape[1]
    chunks = [(c0, min(ck_max, n_keys - c0)) for c0 in range(0, n_keys, ck_max)]
    strips = [slice(j, min(j + MXU_COLS, tq)) for j in range(0, tq, MXU_COLS)]
    lane = lax.broadcasted_iota(jnp.int32, qblk.shape, 1)
    qms = [jnp.where((lane >= hh * HEAD_DIM) & (lane < (hh + 1) * HEAD_DIM), qblk, jnp.zeros_like(qblk))
           for hh in range(N_KV_HEADS)]
    ub = ub_ref[0]
    items = [(chunk, hh, sl) for chunk in chunks for hh in range(N_KV_HEADS) for sl in strips]

    def probs(item):
        (c0, ck), hh, sl = item
        s = lax.dot_general(k_ref[0, c0:c0 + ck, :], qms[hh][sl, :], (((1,), (1,)), ((), ())),
                            preferred_element_type=F32)
        return jnp.exp2(s - ub).astype(BF16)

    acc = {}

    def accumulate(item, p):
        (c0, ck), hh, sl = item
        vc = vt_ref[0, hh * HEAD_DIM:(hh + 1) * HEAD_DIM, c0:c0 + ck]
        vaug = jnp.concatenate([vc, jnp.ones((BF16_ROWS, ck), BF16)], axis=0)
        part = jnp.dot(vaug, p, preferred_element_type=F32)
        key = (hh, sl.start)
        acc[key] = part if key not in acc else acc[key] + part

    p = probs(items[0])
    for i, item in enumerate(items):
        p_next = probs(items[i + 1]) if i + 1 < len(items) else None
        accumulate(item, p)
        p = p_next

    for hh in range(N_KV_HEADS):
        for sl in strips:
            a = acc[(hh, sl.start)]
            o = a[:HEAD_DIM] / a[HEAD_DIM:HEAD_DIM + 1]
            o_ref[0, hh * HEAD_DIM:(hh + 1) * HEAD_DIM, sl] = o.astype(BF16)


def _attention_bounded(ub, q, k, vt, *, tq, ck):
    b, n, aw = q.shape
    groups = aw // LANES
    _, n_keys, kw = k.shape
    return pl.pallas_call(
        functools.partial(_attn_bounded_kernel, ck_max=min(ck, n_keys)),
        out_shape=jax.ShapeDtypeStruct((b, aw, n), BF16),
        grid=(b, groups, n // tq),
        in_specs=[
            pl.BlockSpec(memory_space=pltpu.SMEM),
            pl.BlockSpec((1, tq, LANES), lambda bi, j, qi: (bi, qi, j)),
            pl.BlockSpec((1, n_keys, kw), lambda bi, j, qi: (bi, 0, 0)),
            pl.BlockSpec((1, kw, n_keys), lambda bi, j, qi: (bi, 0, 0)),
        ],
        out_specs=pl.BlockSpec((1, LANES, tq), lambda bi, j, qi: (bi, j, qi)),
        compiler_params=_params("parallel", "parallel", "parallel"),
        name="attention_bounded",
    )(ub, q, k, vt)


def _attention(q, k, vt, *, tq, ck):
    b, n, aw = q.shape
    groups = aw // LANES
    _, n_keys, kw = k.shape
    ck = min(ck, n_keys)
    return pl.pallas_call(
        _attn_kernel,
        out_shape=jax.ShapeDtypeStruct((b, aw, n), BF16),
        grid=(b, groups, n // tq),
        in_specs=[
            pl.BlockSpec((1, tq, LANES), lambda bi, j, qi: (bi, qi, j)),
            pl.BlockSpec((1, n_keys, kw), lambda bi, j, qi: (bi, 0, 0)),
            pl.BlockSpec((1, kw, n_keys), lambda bi, j, qi: (bi, 0, 0)),
        ],
        out_specs=pl.BlockSpec((1, LANES, tq), lambda bi, j, qi: (bi, j, qi)),
        scratch_shapes=[pltpu.VMEM((N_KV_HEADS, 1, tq), F32),
                        pltpu.VMEM((N_KV_HEADS, HEAD_DIM + BF16_ROWS, tq), F32),
                        pltpu.VMEM((N_KV_HEADS, ck, tq), F32),
                        pltpu.VMEM((N_KV_HEADS, ck, tq), F32),
                        pltpu.VMEM((N_KV_HEADS, 1, tq), F32),
                        pltpu.VMEM((N_KV_HEADS, 1, tq), F32)],
        compiler_params=_params("parallel", "parallel", "parallel"),
        name="attention_lat" if n_keys > n else "attention_ctx",
    )(q, k, vt)


CONV_HALO = 16
CONV_ROWS = 32


def _conv_kernel(y_ref, yp_ref, yn_ref, w_ref, b_ref, g_ref, be_ref, o_ref, buf_ref, sh_ref):
    i = pl.program_id(1)
    last = pl.num_programs(1) - 1
    tm = y_ref.shape[1]
    prev = yp_ref[0].astype(F32)
    nxt = yn_ref[0].astype(F32)
    buf_ref[0:CONV_HALO, :] = jnp.where(i > 0, prev, jnp.zeros_like(prev))
    buf_ref[CONV_HALO:CONV_HALO + tm, :] = y_ref[0].astype(F32)
    buf_ref[CONV_HALO + tm:, :] = jnp.where(i < last, nxt, jnp.zeros_like(nxt))
    span = sh_ref.shape[1]
    for res in range(1, SUBLANES):
        sh_ref[res - 1] = buf_ref[res:res + span, :]
    w = w_ref[...]
    off = CONV_HALO - CONV_K // 2
    for r in range(tm // CONV_ROWS):
        r0 = r * CONV_ROWS
        acc = jnp.zeros((CONV_ROWS, w.shape[1]), F32)
        for k in range(CONV_K):
            res = (k + off) % SUBLANES
            base = r0 + k + off - res
            if res == 0:
                tap = buf_ref[base:base + CONV_ROWS, :]
            else:
                tap = sh_ref[res - 1, base:base + CONV_ROWS, :]
            acc = acc + w[k:k + 1, :] * tap
        acc = acc + b_ref[...]
        mu = jnp.mean(acc, axis=-1, keepdims=True)
        cen = acc - mu
        var = jnp.mean(cen * cen, axis=-1, keepdims=True)
        z = cen * lax.rsqrt(var + EPS) * g_ref[...] + be_ref[...]
        o_ref[0, r0:r0 + CONV_ROWS, :] = _silu(z).astype(BF16)


def _conformer_conv(y, w_dw, b_dw, ln_g, ln_b, *, tm):
    b, n, cw = y.shape
    hb = tm // CONV_HALO
    nh = n // CONV_HALO
    vec = pl.BlockSpec((1, cw), lambda bi, i: (0, 0))
    return pl.pallas_call(
        _conv_kernel,
        out_shape=jax.ShapeDtypeStruct((b, n, cw), BF16),
        grid=(b, n // tm),
        in_specs=[
            pl.BlockSpec((1, tm, cw), lambda bi, i: (bi, i, 0)),
            pl.BlockSpec((1, CONV_HALO, cw), lambda bi, i: (bi, jnp.maximum(i * hb - 1, 0), 0)),
            pl.BlockSpec((1, CONV_HALO, cw), lambda bi, i: (bi, jnp.minimum((i + 1) * hb, nh - 1), 0)),
            pl.BlockSpec((CONV_K, cw), lambda bi, i: (0, 0)),
            vec, vec, vec,
        ],
        out_specs=pl.BlockSpec((1, tm, cw), lambda bi, i: (bi, i, 0)),
        scratch_shapes=[pltpu.VMEM((tm + 2 * CONV_HALO, cw), F32),
                        pltpu.VMEM((SUBLANES - 1, tm + 2 * CONV_HALO - SUBLANES, cw), F32)],
        compiler_params=_params("parallel", "parallel"),
        name="conformer_conv",
    )(y, y, y, w_dw, b_dw.reshape(1, cw), ln_g.reshape(1, cw), ln_b.reshape(1, cw))


def _outproj_kernel(at_ref, c_ref, wa_ref, wc_ref, x_ref, g_ref, o_ref):
    o = lax.dot_general(at_ref[0], wa_ref[...], (((0,), (0,)), ((), ())),
                        preferred_element_type=F32)
    o = o + jnp.dot(c_ref[0], wc_ref[...], preferred_element_type=F32)
    o_ref[0] = x_ref[0] + g_ref[0] * o


def _outproj(a_t, c, w_a, w_c, x, gate, *, tm):
    b, n, d = x.shape
    aw = a_t.shape[1]
    cw = c.shape[2]
    bm = gate.shape[0]
    mod_map = (lambda bi, i: (bi, 0, 0)) if bm > 1 else (lambda bi, i: (0, 0, 0))
    return pl.pallas_call(
        _outproj_kernel,
        out_shape=jax.ShapeDtypeStruct((b, n, d), F32),
        grid=(b, n // tm),
        in_specs=[
            pl.BlockSpec((1, aw, tm), lambda bi, i: (bi, 0, i)),
            pl.BlockSpec((1, tm, cw), lambda bi, i: (bi, i, 0)),
            pl.BlockSpec((aw, d), lambda bi, i: (0, 0)),
            pl.BlockSpec((cw, d), lambda bi, i: (0, 0)),
            pl.BlockSpec((1, tm, d), lambda bi, i: (bi, i, 0)),
            pl.BlockSpec((1, 1, d), mod_map),
        ],
        out_specs=pl.BlockSpec((1, tm, d), lambda bi, i: (bi, i, 0)),
        compiler_params=_params("parallel", "parallel"),
        name="outproj",
    )(a_t, c, w_a, w_c, x, gate)


def _swiglu(h, w1_ref, w3_ref, w2_ref, ff_chunk):
    d_ff = w1_ref.shape[2]
    f = None
    for c0 in range(0, d_ff, ff_chunk):
        a = jnp.dot(h, w1_ref[0, :, c0:c0 + ff_chunk], preferred_element_type=F32)
        b = jnp.dot(h, w3_ref[0, :, c0:c0 + ff_chunk], preferred_element_type=F32)
        z = (_silu(a) * b).astype(BF16)
        part = jnp.dot(z, w2_ref[0, c0:c0 + ff_chunk, :], preferred_element_type=F32)
        f = part if f is None else f + part
    return f


def _ffn_kernel(x_ref, sh_ref, sc_ref, g_ref, ng_ref, w1_ref, w3_ref, w2_ref, o_ref, *, ff_chunk):
    h = _norm_mod(x_ref[0], ng_ref[...], sc_ref[0], sh_ref[0]).astype(BF16)
    o_ref[0] = x_ref[0] + g_ref[0] * _swiglu(h, w1_ref, w3_ref, w2_ref, ff_chunk)


def _ffn(x, sh, sc, gate, ng, w1, w3, w2, *, tm):
    b, n, d = x.shape
    d_ff = w1.shape[2]
    bm = sh.shape[0]
    mod_map = (lambda bi, i: (bi, 0, 0)) if bm > 1 else (lambda bi, i: (0, 0, 0))
    mod = pl.BlockSpec((1, 1, d), mod_map)
    return pl.pallas_call(
        functools.partial(_ffn_kernel, ff_chunk=d_ff // 2),
        out_shape=jax.ShapeDtypeStruct((b, n, d), F32),
        grid=(b, n // tm),
        in_specs=[
            pl.BlockSpec((1, tm, d), lambda bi, i: (bi, i, 0)),
            mod, mod, mod,
            pl.BlockSpec((1, d), lambda bi, i: (0, 0)),
            pl.BlockSpec((1, d, d_ff), lambda bi, i: (0, 0, 0)),
            pl.BlockSpec((1, d, d_ff), lambda bi, i: (0, 0, 0)),
            pl.BlockSpec((1, d_ff, d), lambda bi, i: (0, 0, 0)),
        ],
        out_specs=pl.BlockSpec((1, tm, d), lambda bi, i: (bi, i, 0)),
        compiler_params=_params("parallel", "parallel"),
        name="dense_ffn",
    )(x, sh, sc, gate, ng, w1, w3, w2)


META_G1, META_G2, META_E1, META_E2, META_R1, META_R2 = range(6)
TOK_ROWS = 8


def _to_token_tiles(ref, val, lead=()):
    n = val.shape[0]
    for c in range(TOK_ROWS):
        ref[lead + (pl.ds(c, n, stride=TOK_ROWS), slice(None))] = val[:, c * LANES:(c + 1) * LANES]


def _from_token_tiles(ref, n, lead=()):
    return jnp.concatenate([ref[lead + (pl.ds(c, n, stride=TOK_ROWS), slice(None))]
                            for c in range(TOK_ROWS)], axis=1)


def _router_kernel(x_ref, sh_ref, sc_ref, ng_ref, wr_ref, br_ref, tri_ref,
                   h_ref, meta_ref, cnt_ref, carry_ref):
    i = pl.program_id(0)

    @pl.when(i == 0)
    def _():
        carry_ref[...] = jnp.zeros(carry_ref.shape, F32)

    h = _norm_mod(x_ref[...], ng_ref[...], sc_ref[0], sh_ref[0])
    _to_token_tiles(h_ref, h)
    w = wr_ref[...]
    w_hi = w.astype(BF16)
    w_lo = (w - w_hi.astype(F32)).astype(BF16)
    h_hi = h.astype(BF16)
    h_lo = (h - h_hi.astype(F32)).astype(BF16)
    logits = (jnp.dot(h_hi, w_hi, preferred_element_type=F32)
              + (jnp.dot(h_lo, w_hi, preferred_element_type=F32)
                 + jnp.dot(h_hi, w_lo, preferred_element_type=F32))) + br_ref[...]
    lane = lax.broadcasted_iota(jnp.int32, logits.shape, 1)
    t1 = jnp.max(logits, axis=-1, keepdims=True)
    i1 = jnp.min(jnp.where(logits == t1, lane, LANES), axis=-1, keepdims=True)
    rest = jnp.where(lane == i1, NEG_BIG, logits)
    t2 = jnp.max(rest, axis=-1, keepdims=True)
    i2 = jnp.min(jnp.where(rest == t2, lane, LANES), axis=-1, keepdims=True)
    e2 = jnp.exp(t2 - t1)
    den = 1.0 + e2
    onehot = jnp.where((lane == i1) | (lane == i2), 1.0, 0.0)
    before = carry_ref[...] + jnp.dot(tri_ref[...], onehot.astype(BF16), preferred_element_type=F32)
    r1 = jnp.sum(jnp.where(lane == i1, before, 0.0), axis=-1, keepdims=True)
    r2 = jnp.sum(jnp.where(lane == i2, before, 0.0), axis=-1, keepdims=True)
    meta = jnp.zeros(logits.shape, F32)
    for col, val in ((META_G1, 1.0 / den), (META_G2, e2 / den), (META_E1, i1.astype(F32)),
                     (META_E2, i2.astype(F32)), (META_R1, r1), (META_R2, r2)):
        meta = jnp.where(lane == col, val, meta)
    meta_ref[...] = meta
    carry_ref[...] = carry_ref[...] + jnp.sum(onehot, axis=0, keepdims=True)
    cnt_ref[...] = carry_ref[...]


def _router(x2, sh, sc, ng, w_r, b_r, tri, *, tm, rows_per_mod):
    n, d = x2.shape
    per = rows_per_mod // tm
    mod_map = lambda i: (i // per, 0, 0)
    const = lambda i: (0, 0)
    return pl.pallas_call(
        _router_kernel,
        out_shape=(jax.ShapeDtypeStruct((n * TOK_ROWS, LANES), F32),
                   jax.ShapeDtypeStruct((n, LANES), F32),
                   jax.ShapeDtypeStruct((1, LANES), F32)),
        grid=(n // tm,),
        in_specs=[
            pl.BlockSpec((tm, d), lambda i: (i, 0)),
            pl.BlockSpec((1, 1, d), mod_map),
            pl.BlockSpec((1, 1, d), mod_map),
            pl.BlockSpec((1, d), const),
            pl.BlockSpec((d, LANES), const),
            pl.BlockSpec((1, LANES), const),
            pl.BlockSpec((tm, tm), const),
        ],
        out_specs=(pl.BlockSpec((tm * TOK_ROWS, LANES), lambda i: (i, 0)),
                   pl.BlockSpec((tm, LANES), lambda i: (i, 0)),
                   pl.BlockSpec((1, LANES), const)),
        scratch_shapes=[pltpu.VMEM((1, LANES), F32)],
        compiler_params=_params("arbitrary"),
        name="moe_router",
    )(x2, sh, sc, ng, w_r, b_r, tri)


def _tok(ref, t, count=1):
    start = t * TOK_ROWS
    if not isinstance(t, int):
        start = pl.multiple_of(start, TOK_ROWS)
    return ref.at[pl.ds(start, count * TOK_ROWS)]


def _row_copy(src_ref, src_row, dst_ref, dst_row, sem):
    return pltpu.make_async_copy(_tok(src_ref, src_row), _tok(dst_ref, dst_row), sem)


ROW_UNROLL = 8


def _scatter_kernel(tail_ref, pos_ref, h_ref, hs_ref, zero_ref, sem):
    i = pl.program_id(0)
    tm = h_ref.shape[0] // TOK_ROWS
    tile = zero_ref.shape[0] // TOK_ROWS

    @pl.when(i == 0)
    def _():
        zero_ref[...] = jnp.zeros(zero_ref.shape, F32)
        n_tiles = hs_ref.shape[0] // zero_ref.shape[0]
        n_used = tail_ref[N_EXPERTS]

        def zero_tile(first_row):
            return pltpu.make_async_copy(zero_ref, _tok(hs_ref, first_row, tile), sem)

        for e in range(N_EXPERTS):
            zero_tile(tail_ref[e]).start()
        lax.fori_loop(n_used, n_tiles, lambda t, c: (zero_tile(t * tile).start(), c)[1], 0)
        for e in range(N_EXPERTS):
            zero_tile(tail_ref[e]).wait()
        lax.fori_loop(n_used, n_tiles, lambda t, c: (zero_tile(t * tile).wait(), c)[1], 0)

    def issue(r, carry):
        for k in range(2):
            _row_copy(h_ref, r, hs_ref, pos_ref[0, 0, k * tm + r], sem).start(priority=k)
        return carry

    lax.fori_loop(0, tm, issue, 0, unroll=ROW_UNROLL)

    def drain(r, carry):
        _row_copy(h_ref, 0, hs_ref, 0, sem).wait()
        return carry

    lax.fori_loop(0, 2 * tm, drain, 0, unroll=ROW_UNROLL)


def _scatter(tails, pos, h, *, tm, n_rows, tile):
    n = h.shape[0] // TOK_ROWS
    return pl.pallas_call(
        _scatter_kernel,
        out_shape=jax.ShapeDtypeStruct((n_rows * TOK_ROWS, LANES), F32),
        grid_spec=pltpu.PrefetchScalarGridSpec(
            num_scalar_prefetch=1,
            grid=(n // tm,),
            in_specs=[
                pl.BlockSpec((1, 1, 2 * tm), lambda i, tails: (i, 0, 0), memory_space=pltpu.SMEM),
                pl.BlockSpec((tm * TOK_ROWS, LANES), lambda i, tails: (i, 0)),
            ],
            out_specs=pl.BlockSpec(memory_space=pl.ANY),
            scratch_shapes=[pltpu.VMEM((tile * TOK_ROWS, LANES), F32), pltpu.SemaphoreType.DMA],
        ),
        compiler_params=_params("arbitrary"),
        name="moe_scatter",
    )(tails, pos, h)


def _expert_kernel(te_ref, nu_ref, hs_ref, w1_ref, w3_ref, w2_ref, y_ref, *, ff_chunk):
    used = pl.program_id(0) < nu_ref[0]

    @pl.when(used)
    def _():
        tile = hs_ref.shape[0] // TOK_ROWS
        h = _from_token_tiles(hs_ref, tile).astype(BF16)
        _to_token_tiles(y_ref, _swiglu(h, w1_ref, w3_ref, w2_ref, ff_chunk))

    @pl.when(jnp.logical_not(used))
    def _():
        y_ref[...] = jnp.zeros(y_ref.shape, F32)


def _experts(tile_e, n_used, hs, w1, w3, w2, *, tile):
    n_rows = hs.shape[0] // TOK_ROWS
    d = w1.shape[1]
    d_ff = w1.shape[2]
    row_map = lambda t, te, nu: (t, 0)
    w_map = lambda t, te, nu: (te[t], 0, 0)
    return pl.pallas_call(
        functools.partial(_expert_kernel, ff_chunk=d_ff // 2),
        out_shape=jax.ShapeDtypeStruct((n_rows * TOK_ROWS, LANES), F32),
        grid_spec=pltpu.PrefetchScalarGridSpec(
            num_scalar_prefetch=2,
            grid=(n_rows // tile,),
            in_specs=[
                pl.BlockSpec((tile * TOK_ROWS, LANES), row_map),
                pl.BlockSpec((1, d, d_ff), w_map),
                pl.BlockSpec((1, d, d_ff), w_map),
                pl.BlockSpec((1, d_ff, d), w_map),
            ],
            out_specs=pl.BlockSpec((tile * TOK_ROWS, LANES), row_map),
        ),
        compiler_params=_params("arbitrary"),
        name="moe_experts",
    )(tile_e, n_used, hs, w1, w3, w2)


def _combine_kernel(pos_ref, pos_next_ref, x_ref, g_ref, meta_ref, y_ref, *rest, final_norm):
    fg_ref = rest[0] if final_norm else None
    o_ref, buf_ref, sems = rest[-3:]
    tm = x_ref.shape[0]
    i = pl.program_id(0)
    slot = i % 2

    def gather(p_ref, dst):
        def issue(r, carry):
            for k in range(2):
                _row_copy(y_ref, p_ref[0, 0, k * tm + r], buf_ref.at[dst, k], r,
                          sems.at[dst]).start(priority=k)
            return carry
        lax.fori_loop(0, tm, issue, 0, unroll=ROW_UNROLL)

    @pl.when(i == 0)
    def _():
        gather(pos_ref, 0)

    @pl.when(i + 1 < pl.num_programs(0))
    def _():
        gather(pos_next_ref, 1 - slot)

    def drain(r, carry):
        _row_copy(y_ref, 0, buf_ref.at[slot, 0], 0, sems.at[slot]).wait()
        return carry

    lax.fori_loop(0, 2 * tm, drain, 0, unroll=ROW_UNROLL)
    meta = meta_ref[...]
    g1 = meta[:, META_G1:META_G1 + 1]
    g2 = meta[:, META_G2:META_G2 + 1]
    y1 = _from_token_tiles(buf_ref, tm, (slot, 0))
    y2 = _from_token_tiles(buf_ref, tm, (slot, 1))
    out = x_ref[...] + g_ref[0] * (g1 * y1 + g2 * y2)
    if final_norm:
        ms = jnp.mean(out * out, axis=-1, keepdims=True)
        out = out * lax.rsqrt(ms + EPS) * fg_ref[...]
    o_ref[...] = out


def _combine(pos, x2, gate, meta, y, final_g=None, *, tm, rows_per_mod):
    n, d = x2.shape
    per = rows_per_mod // tm
    last = n // tm - 1
    in_specs = [
        pl.BlockSpec((1, 1, 2 * tm), lambda i: (i, 0, 0), memory_space=pltpu.SMEM),
        pl.BlockSpec((1, 1, 2 * tm), lambda i: (jnp.minimum(i + 1, last), 0, 0), memory_space=pltpu.SMEM),
        pl.BlockSpec((tm, d), lambda i: (i, 0)),
        pl.BlockSpec((1, 1, d), lambda i: (i // per, 0, 0)),
        pl.BlockSpec((tm, LANES), lambda i: (i, 0)),
        pl.BlockSpec(memory_space=pl.ANY),
    ]
    args = [pos, pos, x2, gate, meta, y]
    if final_g is not None:
        in_specs.append(pl.BlockSpec((1, d), lambda i: (0, 0)))
        args.append(final_g.reshape(1, d))
    return pl.pallas_call(
        functools.partial(_combine_kernel, final_norm=final_g is not None),
        out_shape=jax.ShapeDtypeStruct((n, d), F32),
        grid=(n // tm,),
        in_specs=in_specs,
        out_specs=pl.BlockSpec((tm, d), lambda i: (i, 0)),
        scratch_shapes=[pltpu.VMEM((2, 2, tm * TOK_ROWS, LANES), F32), pltpu.SemaphoreType.DMA((2,))],
        compiler_params=_params("arbitrary"),
        name="moe_combine",
    )(*args)


def _moe(x, sh, sc, gate, ng, w1, w3, w2, w_r, b_r, tri, final_g=None, *, tm, tile):
    b, n, d = x.shape
    rows = b * n
    rows_per_mod = n if sh.shape[0] > 1 else rows
    x2 = x.reshape(rows, d)
    h, meta, cnt = _router(x2, sh, sc, ng, w_r, b_r, tri, tm=tm, rows_per_mod=rows_per_mod)

    counts = cnt[0, :N_EXPERTS].astype(jnp.int32)
    padded = (counts + tile - 1) // tile * tile
    ends = jnp.cumsum(padded)
    starts = ends - padded
    experts = meta[:, META_E1:META_E2 + 1].astype(jnp.int32)
    ranks = meta[:, META_R1:META_R2 + 1].astype(jnp.int32)
    pos = starts[experts] + ranks
    pos = pos.reshape(rows // tm, tm, 2).transpose(0, 2, 1).reshape(rows // tm, 1, 2 * tm)
    n_rows = 2 * rows + N_EXPERTS * tile
    n_tiles = n_rows // tile
    n_used = (ends[-1] // tile).astype(jnp.int32)
    t = jnp.minimum(jnp.arange(n_tiles, dtype=jnp.int32), n_used - 1)
    tile_e = jnp.minimum(jnp.searchsorted(ends, t * tile, side="right"), N_EXPERTS - 1).astype(jnp.int32)
    busiest = jnp.argmax(counts)
    tails = jnp.where(padded > 0, ends - tile, ends[busiest] - tile).astype(jnp.int32)
    tails = jnp.concatenate([tails, n_used.reshape(1)])

    hs = _scatter(tails, pos, h, tm=tm, n_rows=n_rows, tile=tile)
    y = _experts(tile_e, n_used.reshape(1), hs, w1, w3, w2, tile=tile)
    out = _combine(pos, x2, gate, meta, y, final_g, tm=tm, rows_per_mod=rows_per_mod)
    return out.reshape(b, n, d)


def _final_kernel(x_ref, g_ref, o_ref):
    x = x_ref[0]
    ms = jnp.mean(x * x, axis=-1, keepdims=True)
    o_ref[0] = x * lax.rsqrt(ms + EPS) * g_ref[...]


def _final_norm(x, g, *, tm):
    b, n, d = x.shape
    return pl.pallas_call(
        _final_kernel,
        out_shape=jax.ShapeDtypeStruct((b, n, d), F32),
        grid=(b, n // tm),
        in_specs=[pl.BlockSpec((1, tm, d), lambda bi, i: (bi, i, 0)),
                  pl.BlockSpec((1, d), lambda bi, i: (0, 0))],
        out_specs=pl.BlockSpec((1, tm, d), lambda bi, i: (bi, i, 0)),
        compiler_params=_params("parallel", "parallel"),
        name="final_norm",
    )(x, g.reshape(1, d))


def _qk_column_order():
    half = np.concatenate([np.arange(0, HEAD_DIM, 2), np.arange(1, HEAD_DIM, 2)])
    per_group = N_HEADS // N_KV_HEADS
    heads = [h for j in range(per_group) for h in range(j, N_HEADS, per_group)]
    q_cols = np.concatenate([h * HEAD_DIM + half for h in heads])
    k_cols = N_HEADS * HEAD_DIM + np.concatenate([g * HEAD_DIM + half for g in range(N_KV_HEADS)])
    head_rows = np.concatenate([h * HEAD_DIM + np.arange(HEAD_DIM) for h in heads])
    return half, q_cols, k_cols, head_rows


def _rope_tables(n_tokens):
    rows = n_tokens // GRID_W
    row = jnp.repeat(jnp.arange(rows), GRID_W).astype(F32)
    col = jnp.tile(jnp.arange(GRID_W), rows).astype(F32)
    half = HEAD_DIM // 2
    inv = ROPE_THETA ** (-jnp.arange(0, half, 2, dtype=F32) / half)
    ang = jnp.concatenate([row[:, None] * inv, col[:, None] * inv], axis=-1)
    cos, sin = jnp.cos(ang), jnp.sin(ang)
    reps = LANES // HEAD_DIM
    return (jnp.tile(jnp.concatenate([cos, cos], axis=-1), (1, reps)),
            jnp.tile(jnp.concatenate([-sin, sin], axis=-1), (1, reps)))


ATTN_CHUNK = 512
ATTN_MAX_SHIFT = 48.0
MOE_ROWS = 512
MOE_TILE = 512


def _tri(n):
    return jnp.asarray(np.tril(np.ones((n, n), np.float32), -1), BF16)


def _pick(n, pref):
    t = min(n, pref)
    while n % t:
        t //= 2
    return t


def kernel(x, c, ctx, c_ctx, w_ada, b_ada, norm1_g, w_in, q_norm_g, k_norm_g, dw_w, dw_b,
           conv_ln_g, conv_ln_b, w_out, norm2_g, ffn_w1, ffn_w3, ffn_w2, router_w, router_b,
           exp_w1, exp_w3, exp_w2, final_g):
    bsz, s, d = x.shape
    n_ctx = ctx.shape[1]
    depth = w_ada.shape[0]
    aw = N_HEADS * HEAD_DIM
    kw = N_KV_HEADS * HEAD_DIM

    half, q_cols, k_cols, head_rows = _qk_column_order()
    cols = np.concatenate([q_cols, k_cols, np.arange(aw + kw, w_in.shape[2])])
    w_in_p = w_in[:, :, cols].astype(BF16)
    score_scale = HEAD_DIM ** -0.5 * np.log2(np.e)
    score_bound = (HEAD_DIM * score_scale * jnp.max(jnp.abs(q_norm_g), axis=1)
                   * jnp.max(jnp.abs(k_norm_g), axis=1))
    gain = jnp.concatenate([jnp.tile(q_norm_g[:, half], (1, N_HEADS)) * score_scale,
                            jnp.tile(k_norm_g[:, half], (1, N_KV_HEADS))], axis=1)
    w_out_a = w_out[:, head_rows, :].astype(BF16)
    w_out_c = w_out[:, aw:, :].astype(BF16)
    blk = np.kron(np.eye(2 * LANES // HEAD_DIM), np.ones((HEAD_DIM, HEAD_DIM)))
    bd = jnp.asarray(blk, BF16)
    cos, sin = _rope_tables(s)

    pad = (-(bsz + 1)) % 8
    cvec = jnp.concatenate([c, c_ctx[None, :], jnp.zeros((pad, d), F32)], axis=0)
    mods = _modulation(cvec, w_ada, b_ada)

    tm_lat = _pick(s, 1024)
    tq_lat = _pick(s, 512)
    tm_ctx = n_ctx
    tm_conv = _pick(s, 256)
    tm_ffn = _pick(s, 512)

    xc = ctx
    for l in range(depth):
        last = l == depth - 1
        m = [mods[l, :, i * d:(i + 1) * d] for i in range(6)]
        lat = [v[:bsz, None, :] for v in m]
        cx = [v[bsz:bsz + 1, None, :] for v in m]
        ng1 = norm1_g[l].reshape(1, d)
        ng2 = norm2_g[l].reshape(1, d)
        gl = gain[l].reshape(1, aw + kw)

        q, k, vt, y = _inproj(x, lat[0], lat[1], ng1, w_in_p[l], gl, bd, cos, sin, tm=tm_lat)
        qc, kc, vtc, yc = _inproj(xc, cx[0], cx[1], ng1, w_in_p[l], gl, bd, None, None,
                                  tm=tm_ctx)
        k_all = jnp.concatenate([k, kc], axis=1)
        vt_all = jnp.concatenate([vt, vtc], axis=2)
        ub = score_bound[l] * 1.01
        a_t = lax.cond(ub < ATTN_MAX_SHIFT,
                       lambda: _attention_bounded(ub.reshape(1), q, k_all, vt_all, tq=tq_lat, ck=ATTN_CHUNK),
                       lambda: _attention(q, k_all, vt_all, tq=tq_lat, ck=ATTN_CHUNK))
        c_lat = _conformer_conv(y, dw_w[l], dw_b[l], conv_ln_g[l], conv_ln_b[l], tm=tm_conv)
        x = _outproj(a_t, c_lat, w_out_a[l], w_out_c[l], x, lat[2], tm=tm_ffn)
        if not last:
            a_tc = _attention(qc, kc, vtc, tq=n_ctx, ck=ATTN_CHUNK)
            c_c = _conformer_conv(yc, dw_w[l], dw_b[l], conv_ln_g[l], conv_ln_b[l], tm=n_ctx)
            xc = _outproj(a_tc, c_c, w_out_a[l], w_out_c[l], xc, cx[2], tm=n_ctx)

        i = l // 2
        if l % 2 == 0:
            w1 = ffn_w1[i:i + 1].astype(BF16)
            w3 = ffn_w3[i:i + 1].astype(BF16)
            w2 = ffn_w2[i:i + 1].astype(BF16)
            x = _ffn(x, lat[3], lat[4], lat[5], ng2, w1, w3, w2, tm=tm_ffn)
            if not last:
                xc = _ffn(xc, cx[3], cx[4], cx[5], ng2, w1, w3, w2, tm=n_ctx)
        else:
            w1 = exp_w1[i].astype(BF16)
            w3 = exp_w3[i].astype(BF16)
            w2 = exp_w2[i].astype(BF16)
            w_r = jnp.zeros((d, LANES), F32).at[:, :N_EXPERTS].set(router_w[i])
            b_r = jnp.full((1, LANES), NEG_BIG, F32).at[0, :N_EXPERTS].set(router_b[i])
            tm_l = _pick(s, MOE_ROWS)
            x = _moe(x, lat[3], lat[4], lat[5], ng2, w1, w3, w2, w_r, b_r, _tri(tm_l),
                     final_g if last else None, tm=tm_l, tile=MOE_TILE)
            if not last:
                tm_c = _pick(bsz * n_ctx, MOE_ROWS)
                xc = _moe(xc, cx[3], cx[4], cx[5], ng2, w1, w3, w2, w_r, b_r, _tri(tm_c),
                          tm=tm_c, tile=MOE_TILE)
    if depth % 2 == 0:
        return x
    return _final_norm(x, final_g, tm=tm_ffn)
```

```python
import functools

import numpy as np
import jax
import jax.numpy as jnp
from jax import lax
from jax.experimental import pallas as pl
from jax.experimental.pallas import tpu as pltpu

F32 = jnp.float32
BF16 = jnp.bfloat16

N_HEADS = 8
N_KV_HEADS = 2
HEAD_DIM = 64
GRID_W = 64
CONV_K = 31
N_EXPERTS = 8
ROPE_THETA = 10000.0
EPS = 1e-6

LANES = 128
SUBLANES = 8
MXU_COLS = 256
BF16_ROWS = 16
VMEM_LIMIT = 56 * 1024 * 1024
NEG_BIG = -1e30


def _params(*sem):
    return pltpu.CompilerParams(dimension_semantics=sem, vmem_limit_bytes=VMEM_LIMIT)


def _silu(v):
    return v / (1.0 + jnp.exp(-v))


def _mod_kernel(c_ref, w_ref, b_ref, o_ref):
    c = c_ref[...]
    o_ref[0] = jnp.dot(_silu(c), w_ref[0], preferred_element_type=F32,
                       precision=lax.Precision.HIGHEST) + b_ref[0]


def _modulation(cvec, w_ada, b_ada):
    depth, d, n = w_ada.shape
    r = cvec.shape[0]
    tn = 1536
    return pl.pallas_call(
        _mod_kernel,
        out_shape=jax.ShapeDtypeStruct((depth, r, n), F32),
        grid=(depth, n // tn),
        in_specs=[
            pl.BlockSpec((r, d), lambda l, j: (0, 0)),
            pl.BlockSpec((1, d, tn), lambda l, j: (l, 0, j)),
            pl.BlockSpec((1, 1, tn), lambda l, j: (l, 0, j)),
        ],
        out_specs=pl.BlockSpec((1, r, tn), lambda l, j: (l, 0, j)),
        compiler_params=_params("parallel", "parallel"),
        name="modulation",
    )(cvec, w_ada, b_ada.reshape(depth, 1, n))


def _norm_mod(x, ng, sc, sh):
    ms = jnp.mean(x * x, axis=-1, keepdims=True)
    return (x * lax.rsqrt(ms + EPS)) * ng * (1.0 + sc) + sh


def _inproj_kernel(*refs, use_rope):
    if use_rope:
        (x_ref, sh_ref, sc_ref, ng_ref, w_ref, gain_ref, bd_ref, cos_ref, sin_ref,
         q_ref, k_ref, vt_ref, y_ref) = refs
    else:
        (x_ref, sh_ref, sc_ref, ng_ref, w_ref, gain_ref, bd_ref,
         q_ref, k_ref, vt_ref, y_ref) = refs
    aw = N_HEADS * HEAD_DIM
    kw = N_KV_HEADS * HEAD_DIM
    qkw = aw + kw
    x = x_ref[0]
    tm = x.shape[0]
    h = _norm_mod(x, ng_ref[...], sc_ref[0], sh_ref[0]).astype(BF16)
    res = jnp.dot(h, w_ref[...], preferred_element_type=F32)

    qk = res[:, :qkw]
    sq = (qk * qk).astype(BF16)
    bd = bd_ref[...]
    parts = []
    for c0 in range(0, qkw, 2 * LANES):
        w = min(2 * LANES, qkw - c0)
        parts.append(jnp.dot(sq[:, c0:c0 + w], bd[:w, :w], preferred_element_type=F32))
    ss = jnp.concatenate(parts, axis=1)
    qk = qk * lax.rsqrt(ss * (1.0 / HEAD_DIM) + EPS) * gain_ref[...]

    if use_rope:
        cos = cos_ref[...]
        sin = sin_ref[...]
        lane = lax.broadcasted_iota(jnp.int32, (tm, LANES), 1)
        first = (lane % HEAD_DIM) < (HEAD_DIM // 2)
    outs = []
    for g in range(qkw // LANES):
        blk = qk[:, g * LANES:(g + 1) * LANES]
        if use_rope:
            half = HEAD_DIM // 2
            swapped = jnp.where(first, pltpu.roll(blk, LANES - half, 1), pltpu.roll(blk, half, 1))
            blk = blk * cos + swapped * sin
        outs.append(blk)
    q_ref[0] = jnp.concatenate(outs[:aw // LANES], axis=1).astype(BF16)
    k_ref[0] = outs[aw // LANES].astype(BF16)

    v = res[:, qkw:qkw + kw]
    vt_ref[0] = v.T.astype(BF16)

    cw = (res.shape[1] - qkw - kw) // 2
    a = res[:, qkw + kw:qkw + kw + cw]
    gt = res[:, qkw + kw + cw:]
    y_ref[0] = (a / (1.0 + jnp.exp(-gt))).astype(BF16)


def _inproj(x, sh, sc, ng, w, gain, bd, cos, sin, *, tm):
    b, n, d = x.shape
    in_w = w.shape[1]
    aw = N_HEADS * HEAD_DIM
    kw = N_KV_HEADS * HEAD_DIM
    cw = (in_w - aw - 2 * kw) // 2
    use_rope = cos is not None
    bm = sh.shape[0]
    mod_map = (lambda bi, i: (bi, 0, 0)) if bm > 1 else (lambda bi, i: (0, 0, 0))
    const2 = lambda bi, i: (0, 0)
    in_specs = [
        pl.BlockSpec((1, tm, d), lambda bi, i: (bi, i, 0)),
        pl.BlockSpec((1, 1, d), mod_map),
        pl.BlockSpec((1, 1, d), mod_map),
        pl.BlockSpec((1, d), const2),
        pl.BlockSpec((d, in_w), const2),
        pl.BlockSpec((1, aw + kw), const2),
        pl.BlockSpec((2 * LANES, 2 * LANES), const2),
    ]
    args = [x, sh, sc, ng, w, gain, bd]
    if use_rope:
        in_specs += [pl.BlockSpec((tm, LANES), lambda bi, i: (i, 0))] * 2
        args += [cos, sin]
    out_shape = (
        jax.ShapeDtypeStruct((b, n, aw), BF16),
        jax.ShapeDtypeStruct((b, n, kw), BF16),
        jax.ShapeDtypeStruct((b, kw, n), BF16),
        jax.ShapeDtypeStruct((b, n, cw), BF16),
    )
    out_specs = (
        pl.BlockSpec((1, tm, aw), lambda bi, i: (bi, i, 0)),
        pl.BlockSpec((1, tm, kw), lambda bi, i: (bi, i, 0)),
        pl.BlockSpec((1, kw, tm), lambda bi, i: (bi, 0, i)),
        pl.BlockSpec((1, tm, cw), lambda bi, i: (bi, i, 0)),
    )
    return pl.pallas_call(
        functools.partial(_inproj_kernel, use_rope=use_rope),
        out_shape=out_shape,
        grid=(b, n // tm),
        in_specs=in_specs,
        out_specs=out_specs,
        compiler_params=_params("parallel", "parallel"),
        name="inproj_lat" if use_rope else "inproj_ctx",
    )(*args)


def _attn_kernel(q_ref, k_ref, vt_ref, o_ref, m_ref, acc_ref, sa_ref, sb_ref, ma_ref, mb_ref):
    qblk = q_ref[0]
    n_keys = k_ref.shape[1]
    ck_max = sa_ref.shape[1]
    chunks = [(c0, min(ck_max, n_keys - c0)) for c0 in range(0, n_keys, ck_max)]
    lane = lax.broadcasted_iota(jnp.int32, qblk.shape, 1)
    qms = [jnp.where((lane >= hh * HEAD_DIM) & (lane < (hh + 1) * HEAD_DIM), qblk, jnp.zeros_like(qblk))
           for hh in range(N_KV_HEADS)]
    m_ref[...] = jnp.full(m_ref.shape, NEG_BIG, F32)
    acc_ref[...] = jnp.zeros(acc_ref.shape, F32)

    tq = qblk.shape[0]
    strips = [slice(j, min(j + MXU_COLS, tq)) for j in range(0, tq, MXU_COLS)]

    def scores(chunk, s_ref, mc_ref):
        c0, ck = chunk
        kc = k_ref[0, c0:c0 + ck, :]
        for hh in range(N_KV_HEADS):
            for sl in strips:
                s = lax.dot_general(kc, qms[hh][sl, :], (((1,), (1,)), ((), ())),
                                    preferred_element_type=F32)
                s_ref[hh, :ck, sl] = s
                mc_ref[hh, :, sl] = jnp.max(s, axis=0, keepdims=True)

    def consume(chunk, s_ref, mc_ref):
        c0, ck = chunk
        for hh in range(N_KV_HEADS):
            vc = vt_ref[0, hh * HEAD_DIM:(hh + 1) * HEAD_DIM, c0:c0 + ck]
            vaug = jnp.concatenate([vc, jnp.ones((BF16_ROWS, ck), BF16)], axis=0)
            for sl in strips:
                m_old = m_ref[hh, :, sl]
                m_new = jnp.maximum(m_old, mc_ref[hh, :, sl])
                alpha = jnp.exp2(m_old - m_new)
                p = jnp.exp2(s_ref[hh, :ck, sl] - m_new).astype(BF16)
                acc_ref[hh, :, sl] = (acc_ref[hh, :, sl] * alpha
                                      + jnp.dot(vaug, p, preferred_element_type=F32))
                m_ref[hh, :, sl] = m_new

    bufs = ((sa_ref, ma_ref), (sb_ref, mb_ref))
    scores(chunks[0], *bufs[0])
    for c, chunk in enumerate(chunks):
        if c + 1 < len(chunks):
            scores(chunks[c + 1], *bufs[(c + 1) % 2])
        consume(chunk, *bufs[c % 2])

    for hh in range(N_KV_HEADS):
        acc = acc_ref[hh]
        o = acc[:HEAD_DIM] / acc[HEAD_DIM:HEAD_DIM + 1]
        o_ref[0, hh * HEAD_DIM:(hh + 1) * HEAD_DIM, :] = o.astype(BF16)


def _attn_bounded_kernel(ub_ref, q_ref, k_ref, vt_ref, o_ref, *, ck_max):
    qblk = q_ref[0]
    tq = qblk.shape[0]
    n_keys = k_ref.shape[1]
    chunks = [(c0, min(ck_max, n_keys - c0)) for c0 in range(0, n_keys, ck_max)]
    strips = [slice(j, min(j + MXU_COLS, tq)) for j in range(0, tq, MXU_COLS)]
    lane = lax.broadcasted_iota(jnp.int32, qblk.shape, 1)
    qms = [jnp.where((lane >= hh * HEAD_DIM) & (lane < (hh + 1) * HEAD_DIM), qblk, jnp.zeros_like(qblk))
           for hh in range(N_KV_HEADS)]
    ub = ub_ref[0]
    items = [(chunk, hh, sl) for chunk in chunks for hh in range(N_KV_HEADS) for sl in strips]

    def probs(item):
        (c0, ck), hh, sl = item
        s = lax.dot_general(k_ref[0, c0:c0 + ck, :], qms[hh][sl, :], (((1,), (1,)), ((), ())),
                            preferred_element_type=F32)
        return jnp.exp2(s - ub).astype(BF16)

    acc = {}

    def accumulate(item, p):
        (c0, ck), hh, sl = item
        vc = vt_ref[0, hh * HEAD_DIM:(hh + 1) * HEAD_DIM, c0:c0 + ck]
        vaug = jnp.concatenate([vc, jnp.ones((BF16_ROWS, ck), BF16)], axis=0)
        part = jnp.dot(vaug, p, preferred_element_type=F32)
        key = (hh, sl.start)
        acc[key] = part if key not in acc else acc[key] + part

    p = probs(items[0])
    for i, item in enumerate(items):
        p_next = probs(items[i + 1]) if i + 1 < len(items) else None
        accumulate(item, p)
        p = p_next

    for hh in range(N_KV_HEADS):
        for sl in strips:
            a = acc[(hh, sl.start)]
            o = a[:HEAD_DIM] / a[HEAD_DIM:HEAD_DIM + 1]
            o_ref[0, hh * HEAD_DIM:(hh + 1) * HEAD_DIM, sl] = o.astype(BF16)


def _attention_bounded(ub, q, k, vt, *, tq, ck):
    b, n, aw = q.shape
    groups = aw // LANES
    _, n_keys, kw = k.shape
    return pl.pallas_call(
        functools.partial(_attn_bounded_kernel, ck_max=min(ck, n_keys)),
        out_shape=jax.ShapeDtypeStruct((b, aw, n), BF16),
        grid=(b, groups, n // tq),
        in_specs=[
            pl.BlockSpec(memory_space=pltpu.SMEM),
            pl.BlockSpec((1, tq, LANES), lambda bi, j, qi: (bi, qi, j)),
            pl.BlockSpec((1, n_keys, kw), lambda bi, j, qi: (bi, 0, 0)),
            pl.BlockSpec((1, kw, n_keys), lambda bi, j, qi: (bi, 0, 0)),
        ],
        out_specs=pl.BlockSpec((1, LANES, tq), lambda bi, j, qi: (bi, j, qi)),
        compiler_params=_params("parallel", "parallel", "parallel"),
        name="attention_bounded",
    )(ub, q, k, vt)


def _attention(q, k, vt, *, tq, ck):
    b, n, aw = q.shape
    groups = aw // LANES
    _, n_keys, kw = k.shape
    ck = min(ck, n_keys)
    return pl.pallas_call(
        _attn_kernel,
        out_shape=jax.ShapeDtypeStruct((b, aw, n), BF16),
        grid=(b, groups, n // tq),
        in_specs=[
            pl.BlockSpec((1, tq, LANES), lambda bi, j, qi: (bi, qi, j)),
            pl.BlockSpec((1, n_keys, kw), lambda bi, j, qi: (bi, 0, 0)),
            pl.BlockSpec((1, kw, n_keys), lambda bi, j, qi: (bi, 0, 0)),
        ],
        out_specs=pl.BlockSpec((1, LANES, tq), lambda bi, j, qi: (bi, j, qi)),
        scratch_shapes=[pltpu.VMEM((N_KV_HEADS, 1, tq), F32),
                        pltpu.VMEM((N_KV_HEADS, HEAD_DIM + BF16_ROWS, tq), F32),
                        pltpu.VMEM((N_KV_HEADS, ck, tq), F32),
                        pltpu.VMEM((N_KV_HEADS, ck, tq), F32),
                        pltpu.VMEM((N_KV_HEADS, 1, tq), F32),
                        pltpu.VMEM((N_KV_HEADS, 1, tq), F32)],
        compiler_params=_params("parallel", "parallel", "parallel"),
        name="attention_lat" if n_keys > n else "attention_ctx",
    )(q, k, vt)


CONV_HALO = 16
CONV_ROWS = 32


def _conv_kernel(y_ref, yp_ref, yn_ref, w_ref, b_ref, g_ref, be_ref, o_ref, buf_ref, sh_ref):
    i = pl.program_id(1)
    last = pl.num_programs(1) - 1
    tm = y_ref.shape[1]
    prev = yp_ref[0].astype(F32)
    nxt = yn_ref[0].astype(F32)
    buf_ref[0:CONV_HALO, :] = jnp.where(i > 0, prev, jnp.zeros_like(prev))
    buf_ref[CONV_HALO:CONV_HALO + tm, :] = y_ref[0].astype(F32)
    buf_ref[CONV_HALO + tm:, :] = jnp.where(i < last, nxt, jnp.zeros_like(nxt))
    span = sh_ref.shape[1]
    for res in range(1, SUBLANES):
        sh_ref[res - 1] = buf_ref[res:res + span, :]
    w = w_ref[...]
    off = CONV_HALO - CONV_K // 2
    for r in range(tm // CONV_ROWS):
        r0 = r * CONV_ROWS
        acc = jnp.zeros((CONV_ROWS, w.shape[1]), F32)
        for k in range(CONV_K):
            res = (k + off) % SUBLANES
            base = r0 + k + off - res
            if res == 0:
                tap = buf_ref[base:base + CONV_ROWS, :]
            else:
                tap = sh_ref[res - 1, base:base + CONV_ROWS, :]
            acc = acc + w[k:k + 1, :] * tap
        acc = acc + b_ref[...]
        mu = jnp.mean(acc, axis=-1, keepdims=True)
        cen = acc - mu
        var = jnp.mean(cen * cen, axis=-1, keepdims=True)
        z = cen * lax.rsqrt(var + EPS) * g_ref[...] + be_ref[...]
        o_ref[0, r0:r0 + CONV_ROWS, :] = _silu(z).astype(BF16)


def _conformer_conv(y, w_dw, b_dw, ln_g, ln_b, *, tm):
    b, n, cw = y.shape
    hb = tm // CONV_HALO
    nh = n // CONV_HALO
    vec = pl.BlockSpec((1, cw), lambda bi, i: (0, 0))
    return pl.pallas_call(
        _conv_kernel,
        out_shape=jax.ShapeDtypeStruct((b, n, cw), BF16),
        grid=(b, n // tm),
        in_specs=[
            pl.BlockSpec((1, tm, cw), lambda bi, i: (bi, i, 0)),
            pl.BlockSpec((1, CONV_HALO, cw), lambda bi, i: (bi, jnp.maximum(i * hb - 1, 0), 0)),
            pl.BlockSpec((1, CONV_HALO, cw), lambda bi, i: (bi, jnp.minimum((i + 1) * hb, nh - 1), 0)),
            pl.BlockSpec((CONV_K, cw), lambda bi, i: (0, 0)),
            vec, vec, vec,
        ],
        out_specs=pl.BlockSpec((1, tm, cw), lambda bi, i: (bi, i, 0)),
        scratch_shapes=[pltpu.VMEM((tm + 2 * CONV_HALO, cw), F32),
                        pltpu.VMEM((SUBLANES - 1, tm + 2 * CONV_HALO - SUBLANES, cw), F32)],
        compiler_params=_params("parallel", "parallel"),
        name="conformer_conv",
    )(y, y, y, w_dw, b_dw.reshape(1, cw), ln_g.reshape(1, cw), ln_b.reshape(1, cw))


def _outproj_kernel(at_ref, c_ref, wa_ref, wc_ref, x_ref, g_ref, o_ref):
    o = lax.dot_general(at_ref[0], wa_ref[...], (((0,), (0,)), ((), ())),
                        preferred_element_type=F32)
    o = o + jnp.dot(c_ref[0], wc_ref[...], preferred_element_type=F32)
    o_ref[0] = x_ref[0] + g_ref[0] * o


def _outproj(a_t, c, w_a, w_c, x, gate, *, tm):
    b, n, d = x.shape
    aw = a_t.shape[1]
    cw = c.shape[2]
    bm = gate.shape[0]
    mod_map = (lambda bi, i: (bi, 0, 0)) if bm > 1 else (lambda bi, i: (0, 0, 0))
    return pl.pallas_call(
        _outproj_kernel,
        out_shape=jax.ShapeDtypeStruct((b, n, d), F32),
        grid=(b, n // tm),
        in_specs=[
            pl.BlockSpec((1, aw, tm), lambda bi, i: (bi, 0, i)),
            pl.BlockSpec((1, tm, cw), lambda bi, i: (bi, i, 0)),
            pl.BlockSpec((aw, d), lambda bi, i: (0, 0)),
            pl.BlockSpec((cw, d), lambda bi, i: (0, 0)),
            pl.BlockSpec((1, tm, d), lambda bi, i: (bi, i, 0)),
            pl.BlockSpec((1, 1, d), mod_map),
        ],
        out_specs=pl.BlockSpec((1, tm, d), lambda bi, i: (bi, i, 0)),
        compiler_params=_params("parallel", "parallel"),
        name="outproj",
    )(a_t, c, w_a, w_c, x, gate)


def _swiglu(h, w1_ref, w3_ref, w2_ref, ff_chunk):
    d_ff = w1_ref.shape[2]
    f = None
    for c0 in range(0, d_ff, ff_chunk):
        a = jnp.dot(h, w1_ref[0, :, c0:c0 + ff_chunk], preferred_element_type=F32)
        b = jnp.dot(h, w3_ref[0, :, c0:c0 + ff_chunk], preferred_element_type=F32)
        z = (_silu(a) * b).astype(BF16)
        part = jnp.dot(z, w2_ref[0, c0:c0 + ff_chunk, :], preferred_element_type=F32)
        f = part if f is None else f + part
    return f


def _ffn_kernel(x_ref, sh_ref, sc_ref, g_ref, ng_ref, w1_ref, w3_ref, w2_ref, o_ref, *, ff_chunk):
    h = _norm_mod(x_ref[0], ng_ref[...], sc_ref[0], sh_ref[0]).astype(BF16)
    o_ref[0] = x_ref[0] + g_ref[0] * _swiglu(h, w1_ref, w3_ref, w2_ref, ff_chunk)


def _ffn(x, sh, sc, gate, ng, w1, w3, w2, *, tm):
    b, n, d = x.shape
    d_ff = w1.shape[2]
    bm = sh.shape[0]
    mod_map = (lambda bi, i: (bi, 0, 0)) if bm > 1 else (lambda bi, i: (0, 0, 0))
    mod = pl.BlockSpec((1, 1, d), mod_map)
    return pl.pallas_call(
        functools.partial(_ffn_kernel, ff_chunk=d_ff // 2),
        out_shape=jax.ShapeDtypeStruct((b, n, d), F32),
        grid=(b, n // tm),
        in_specs=[
            pl.BlockSpec((1, tm, d), lambda bi, i: (bi, i, 0)),
            mod, mod, mod,
            pl.BlockSpec((1, d), lambda bi, i: (0, 0)),
            pl.BlockSpec((1, d, d_ff), lambda bi, i: (0, 0, 0)),
            pl.BlockSpec((1, d, d_ff), lambda bi, i: (0, 0, 0)),
            pl.BlockSpec((1, d_ff, d), lambda bi, i: (0, 0, 0)),
        ],
        out_specs=pl.BlockSpec((1, tm, d), lambda bi, i: (bi, i, 0)),
        compiler_params=_params("parallel", "parallel"),
        name="dense_ffn",
    )(x, sh, sc, gate, ng, w1, w3, w2)


META_G1, META_G2, META_E1, META_E2, META_R1, META_R2 = range(6)
TOK_ROWS = 8


def _to_token_tiles(ref, val, lead=()):
    n = val.shape[0]
    for c in range(TOK_ROWS):
        ref[lead + (pl.ds(c, n, stride=TOK_ROWS), slice(None))] = val[:, c * LANES:(c + 1) * LANES]


def _from_token_tiles(ref, n, lead=()):
    return jnp.concatenate([ref[lead + (pl.ds(c, n, stride=TOK_ROWS), slice(None))]
                            for c in range(TOK_ROWS)], axis=1)


def _router_kernel(x_ref, sh_ref, sc_ref, ng_ref, wr_ref, br_ref, tri_ref,
                   h_ref, meta_ref, cnt_ref, carry_ref):
    i = pl.program_id(0)

    @pl.when(i == 0)
    def _():
        carry_ref[...] = jnp.zeros(carry_ref.shape, F32)

    h = _norm_mod(x_ref[...], ng_ref[...], sc_ref[0], sh_ref[0])
    _to_token_tiles(h_ref, h)
    w = wr_ref[...]
    w_hi = w.astype(BF16)
    w_lo = (w - w_hi.astype(F32)).astype(BF16)
    h_hi = h.astype(BF16)
    h_lo = (h - h_hi.astype(F32)).astype(BF16)
    logits = (jnp.dot(h_hi, w_hi, preferred_element_type=F32)
              + (jnp.dot(h_lo, w_hi, preferred_element_type=F32)
                 + jnp.dot(h_hi, w_lo, preferred_element_type=F32))) + br_ref[...]
    lane = lax.broadcasted_iota(jnp.int32, logits.shape, 1)
    t1 = jnp.max(logits, axis=-1, keepdims=True)
    i1 = jnp.min(jnp.where(logits == t1, lane, LANES), axis=-1, keepdims=True)
    rest = jnp.where(lane == i1, NEG_BIG, logits)
    t2 = jnp.max(rest, axis=-1, keepdims=True)
    i2 = jnp.min(jnp.where(rest == t2, lane, LANES), axis=-1, keepdims=True)
    e2 = jnp.exp(t2 - t1)
    den = 1.0 + e2
    onehot = jnp.where((lane == i1) | (lane == i2), 1.0, 0.0)
    before = carry_ref[...] + jnp.dot(tri_ref[...], onehot.astype(BF16), preferred_element_type=F32)
    r1 = jnp.sum(jnp.where(lane == i1, before, 0.0), axis=-1, keepdims=True)
    r2 = jnp.sum(jnp.where(lane == i2, before, 0.0), axis=-1, keepdims=True)
    meta = jnp.zeros(logits.shape, F32)
    for col, val in ((META_G1, 1.0 / den), (META_G2, e2 / den), (META_E1, i1.astype(F32)),
                     (META_E2, i2.astype(F32)), (META_R1, r1), (META_R2, r2)):
        meta = jnp.where(lane == col, val, meta)
    meta_ref[...] = meta
    carry_ref[...] = carry_ref[...] + jnp.sum(onehot, axis=0, keepdims=True)
    cnt_ref[...] = carry_ref[...]


def _router(x2, sh, sc, ng, w_r, b_r, tri, *, tm, rows_per_mod):
    n, d = x2.shape
    per = rows_per_mod // tm
    mod_map = lambda i: (i // per, 0, 0)
    const = lambda i: (0, 0)
    return pl.pallas_call(
        _router_kernel,
        out_shape=(jax.ShapeDtypeStruct((n * TOK_ROWS, LANES), F32),
                   jax.ShapeDtypeStruct((n, LANES), F32),
                   jax.ShapeDtypeStruct((1, LANES), F32)),
        grid=(n // tm,),
        in_specs=[
            pl.BlockSpec((tm, d), lambda i: (i, 0)),
            pl.BlockSpec((1, 1, d), mod_map),
            pl.BlockSpec((1, 1, d), mod_map),
            pl.BlockSpec((1, d), const),
            pl.BlockSpec((d, LANES), const),
            pl.BlockSpec((1, LANES), const),
            pl.BlockSpec((tm, tm), const),
        ],
        out_specs=(pl.BlockSpec((tm * TOK_ROWS, LANES), lambda i: (i, 0)),
                   pl.BlockSpec((tm, LANES), lambda i: (i, 0)),
                   pl.BlockSpec((1, LANES), const)),
        scratch_shapes=[pltpu.VMEM((1, LANES), F32)],
        compiler_params=_params("arbitrary"),
        name="moe_router",
    )(x2, sh, sc, ng, w_r, b_r, tri)


def _tok(ref, t, count=1):
    start = t * TOK_ROWS
    if not isinstance(t, int):
        start = pl.multiple_of(start, TOK_ROWS)
    return ref.at[pl.ds(start, count * TOK_ROWS)]


def _row_copy(src_ref, src_row, dst_ref, dst_row, sem):
    return pltpu.make_async_copy(_tok(src_ref, src_row), _tok(dst_ref, dst_row), sem)


ROW_UNROLL = 8


def _scatter_kernel(tail_ref, pos_ref, h_ref, hs_ref, zero_ref, sem):
    i = pl.program_id(0)
    tm = h_ref.shape[0] // TOK_ROWS
    tile = zero_ref.shape[0] // TOK_ROWS

    @pl.when(i == 0)
    def _():
        zero_ref[...] = jnp.zeros(zero_ref.shape, F32)
        n_tiles = hs_ref.shape[0] // zero_ref.shape[0]
        n_used = tail_ref[N_EXPERTS]

        def zero_tile(first_row):
            return pltpu.make_async_copy(zero_ref, _tok(hs_ref, first_row, tile), sem)

        for e in range(N_EXPERTS):
            zero_tile(tail_ref[e]).start()
        lax.fori_loop(n_used, n_tiles, lambda t, c: (zero_tile(t * tile).start(), c)[1], 0)
        for e in range(N_EXPERTS):
            zero_tile(tail_ref[e]).wait()
        lax.fori_loop(n_used, n_tiles, lambda t, c: (zero_tile(t * tile).wait(), c)[1], 0)

    def issue(r, carry):
        for k in range(2):
            _row_copy(h_ref, r, hs_ref, pos_ref[0, 0, k * tm + r], sem).start(priority=k)
        return carry

    lax.fori_loop(0, tm, issue, 0, unroll=ROW_UNROLL)

    def drain(r, carry):
        _row_copy(h_ref, 0, hs_ref, 0, sem).wait()
        return carry

    lax.fori_loop(0, 2 * tm, drain, 0, unroll=ROW_UNROLL)


def _scatter(tails, pos, h, *, tm, n_rows, tile):
    n = h.shape[0] // TOK_ROWS
    return pl.pallas_call(
        _scatter_kernel,
        out_shape=jax.ShapeDtypeStruct((n_rows * TOK_ROWS, LANES), F32),
        grid_spec=pltpu.PrefetchScalarGridSpec(
            num_scalar_prefetch=1,
            grid=(n // tm,),
            in_specs=[
                pl.BlockSpec((1, 1, 2 * tm), lambda i, tails: (i, 0, 0), memory_space=pltpu.SMEM),
                pl.BlockSpec((tm * TOK_ROWS, LANES), lambda i, tails: (i, 0)),
            ],
            out_specs=pl.BlockSpec(memory_space=pl.ANY),
            scratch_shapes=[pltpu.VMEM((tile * TOK_ROWS, LANES), F32), pltpu.SemaphoreType.DMA],
        ),
        compiler_params=_params("arbitrary"),
        name="moe_scatter",
    )(tails, pos, h)


def _expert_kernel(te_ref, nu_ref, hs_ref, w1_ref, w3_ref, w2_ref, y_ref, *, ff_chunk):
    used = pl.program_id(0) < nu_ref[0]

    @pl.when(used)
    def _():
        tile = hs_ref.shape[0] // TOK_ROWS
        h = _from_token_tiles(hs_ref, tile).astype(BF16)
        _to_token_tiles(y_ref, _swiglu(h, w1_ref, w3_ref, w2_ref, ff_chunk))

    @pl.when(jnp.logical_not(used))
    def _():
        y_ref[...] = jnp.zeros(y_ref.shape, F32)


def _experts(tile_e, n_used, hs, w1, w3, w2, *, tile):
    n_rows = hs.shape[0] // TOK_ROWS
    d = w1.shape[1]
    d_ff = w1.shape[2]
    row_map = lambda t, te, nu: (t, 0)
    w_map = lambda t, te, nu: (te[t], 0, 0)
    return pl.pallas_call(
        functools.partial(_expert_kernel, ff_chunk=d_ff // 2),
        out_shape=jax.ShapeDtypeStruct((n_rows * TOK_ROWS, LANES), F32),
        grid_spec=pltpu.PrefetchScalarGridSpec(
            num_scalar_prefetch=2,
            grid=(n_rows // tile,),
            in_specs=[
                pl.BlockSpec((tile * TOK_ROWS, LANES), row_map),
                pl.BlockSpec((1, d, d_ff), w_map),
                pl.BlockSpec((1, d, d_ff), w_map),
                pl.BlockSpec((1, d_ff, d), w_map),
            ],
            out_specs=pl.BlockSpec((tile * TOK_ROWS, LANES), row_map),
        ),
        compiler_params=_params("arbitrary"),
        name="moe_experts",
    )(tile_e, n_used, hs, w1, w3, w2)


def _combine_kernel(pos_ref, pos_next_ref, x_ref, g_ref, meta_ref, y_ref, *rest, final_norm):
    fg_ref = rest[0] if final_norm else None
    o_ref, buf_ref, sems = rest[-3:]
    tm = x_ref.shape[0]
    i = pl.program_id(0)
    slot = i % 2

    def gather(p_ref, dst):
        def issue(r, carry):
            for k in range(2):
                _row_copy(y_ref, p_ref[0, 0, k * tm + r], buf_ref.at[dst, k], r,
                          sems.at[dst]).start(priority=k)
            return carry
        lax.fori_loop(0, tm, issue, 0, unroll=ROW_UNROLL)

    @pl.when(i == 0)
    def _():
        gather(pos_ref, 0)

    @pl.when(i + 1 < pl.num_programs(0))
    def _():
        gather(pos_next_ref, 1 - slot)

    def drain(r, carry):
        _row_copy(y_ref, 0, buf_ref.at[slot, 0], 0, sems.at[slot]).wait()
        return carry

    lax.fori_loop(0, 2 * tm, drain, 0, unroll=ROW_UNROLL)
    meta = meta_ref[...]
    g1 = meta[:, META_G1:META_G1 + 1]
    g2 = meta[:, META_G2:META_G2 + 1]
    y1 = _from_token_tiles(buf_ref, tm, (slot, 0))
    y2 = _from_token_tiles(buf_ref, tm, (slot, 1))
    out = x_ref[...] + g_ref[0] * (g1 * y1 + g2 * y2)
    if final_norm:
        ms = jnp.mean(out * out, axis=-1, keepdims=True)
        out = out * lax.rsqrt(ms + EPS) * fg_ref[...]
    o_ref[...] = out


def _combine(pos, x2, gate, meta, y, final_g=None, *, tm, rows_per_mod):
    n, d = x2.shape
    per = rows_per_mod // tm
    last = n // tm - 1
    in_specs = [
        pl.BlockSpec((1, 1, 2 * tm), lambda i: (i, 0, 0), memory_space=pltpu.SMEM),
        pl.BlockSpec((1, 1, 2 * tm), lambda i: (jnp.minimum(i + 1, last), 0, 0), memory_space=pltpu.SMEM),
        pl.BlockSpec((tm, d), lambda i: (i, 0)),
        pl.BlockSpec((1, 1, d), lambda i: (i // per, 0, 0)),
        pl.BlockSpec((tm, LANES), lambda i: (i, 0)),
        pl.BlockSpec(memory_space=pl.ANY),
    ]
    args = [pos, pos, x2, gate, meta, y]
    if final_g is not None:
        in_specs.append(pl.BlockSpec((1, d), lambda i: (0, 0)))
        args.append(final_g.reshape(1, d))
    return pl.pallas_call(
        functools.partial(_combine_kernel, final_norm=final_g is not None),
        out_shape=jax.ShapeDtypeStruct((n, d), F32),
        grid=(n // tm,),
        in_specs=in_specs,
        out_specs=pl.BlockSpec((tm, d), lambda i: (i, 0)),
        scratch_shapes=[pltpu.VMEM((2, 2, tm * TOK_ROWS, LANES), F32), pltpu.SemaphoreType.DMA((2,))],
        compiler_params=_params("arbitrary"),
        name="moe_combine",
    )(*args)


def _moe(x, sh, sc, gate, ng, w1, w3, w2, w_r, b_r, tri, final_g=None, *, tm, tile):
    b, n, d = x.shape
    rows = b * n
    rows_per_mod = n if sh.shape[0] > 1 else rows
    x2 = x.reshape(rows, d)
    h, meta, cnt = _router(x2, sh, sc, ng, w_r, b_r, tri, tm=tm, rows_per_mod=rows_per_mod)

    counts = cnt[0, :N_EXPERTS].astype(jnp.int32)
    padded = (counts + tile - 1) // tile * tile
    ends = jnp.cumsum(padded)
    starts = ends - padded
    experts = meta[:, META_E1:META_E2 + 1].astype(jnp.int32)
    ranks = meta[:, META_R1:META_R2 + 1].astype(jnp.int32)
    pos = starts[experts] + ranks
    pos = pos.reshape(rows // tm, tm, 2).transpose(0, 2, 1).reshape(rows // tm, 1, 2 * tm)
    n_rows = 2 * rows + N_EXPERTS * tile
    n_tiles = n_rows // tile
    n_used = (ends[-1] // tile).astype(jnp.int32)
    t = jnp.minimum(jnp.arange(n_tiles, dtype=jnp.int32), n_used - 1)
    tile_e = jnp.minimum(jnp.searchsorted(ends, t * tile, side="right"), N_EXPERTS - 1).astype(jnp.int32)
    busiest = jnp.argmax(counts)
    tails = jnp.where(padded > 0, ends - tile, ends[busiest] - tile).astype(jnp.int32)
    tails = jnp.concatenate([tails, n_used.reshape(1)])

    hs = _scatter(tails, pos, h, tm=tm, n_rows=n_rows, tile=tile)
    y = _experts(tile_e, n_used.reshape(1), hs, w1, w3, w2, tile=tile)
    out = _combine(pos, x2, gate, meta, y, final_g, tm=tm, rows_per_mod=rows_per_mod)
    return out.reshape(b, n, d)


def _final_kernel(x_ref, g_ref, o_ref):
    x = x_ref[0]
    ms = jnp.mean(x * x, axis=-1, keepdims=True)
    o_ref[0] = x * lax.rsqrt(ms + EPS) * g_ref[...]


def _final_norm(x, g, *, tm):
    b, n, d = x.shape
    return pl.pallas_call(
        _final_kernel,
        out_shape=jax.ShapeDtypeStruct((b, n, d), F32),
        grid=(b, n // tm),
        in_specs=[pl.BlockSpec((1, tm, d), lambda bi, i: (bi, i, 0)),
                  pl.BlockSpec((1, d), lambda bi, i: (0, 0))],
        out_specs=pl.BlockSpec((1, tm, d), lambda bi, i: (bi, i, 0)),
        compiler_params=_params("parallel", "parallel"),
        name="final_norm",
    )(x, g.reshape(1, d))


def _qk_column_order():
    half = np.concatenate([np.arange(0, HEAD_DIM, 2), np.arange(1, HEAD_DIM, 2)])
    per_group = N_HEADS // N_KV_HEADS
    heads = [h for j in range(per_group) for h in range(j, N_HEADS, per_group)]
    q_cols = np.concatenate([h * HEAD_DIM + half for h in heads])
    k_cols = N_HEADS * HEAD_DIM + np.concatenate([g * HEAD_DIM + half for g in range(N_KV_HEADS)])
    head_rows = np.concatenate([h * HEAD_DIM + np.arange(HEAD_DIM) for h in heads])
    return half, q_cols, k_cols, head_rows


def _rope_tables(n_tokens):
    rows = n_tokens // GRID_W
    row = jnp.repeat(jnp.arange(rows), GRID_W).astype(F32)
    col = jnp.tile(jnp.arange(GRID_W), rows).astype(F32)
    half = HEAD_DIM // 2
    inv = ROPE_THETA ** (-jnp.arange(0, half, 2, dtype=F32) / half)
    ang = jnp.concatenate([row[:, None] * inv, col[:, None] * inv], axis=-1)
    cos, sin = jnp.cos(ang), jnp.sin(ang)
    reps = LANES // HEAD_DIM
    return (jnp.tile(jnp.concatenate([cos, cos], axis=-1), (1, reps)),
            jnp.tile(jnp.concatenate([-sin, sin], axis=-1), (1, reps)))


ATTN_CHUNK = 512
ATTN_MAX_SHIFT = 48.0
MOE_ROWS = 512
MOE_TILE = 512


def _tri(n):
    return jnp.asarray(np.tril(np.ones((n, n), np.float32), -1), BF16)


def _pick(n, pref):
    t = min(n, pref)
    while n % t:
        t //= 2
    return t


def kernel(x, c, ctx, c_ctx, w_ada, b_ada, norm1_g, w_in, q_norm_g, k_norm_g, dw_w, dw_b,
           conv_ln_g, conv_ln_b, w_out, norm2_g, ffn_w1, ffn_w3, ffn_w2, router_w, router_b,
           exp_w1, exp_w3, exp_w2, final_g):
    bsz, s, d = x.shape
    n_ctx = ctx.shape[1]
    depth = w_ada.shape[0]
    aw = N_HEADS * HEAD_DIM
    kw = N_KV_HEADS * HEAD_DIM

    half, q_cols, k_cols, head_rows = _qk_column_order()
    cols = np.concatenate([q_cols, k_cols, np.arange(aw + kw, w_in.shape[2])])
    w_in_p = w_in[:, :, cols].astype(BF16)
    score_scale = HEAD_DIM ** -0.5 * np.log2(np.e)
    score_bound = (HEAD_DIM * score_scale * jnp.max(jnp.abs(q_norm_g), axis=1)
                   * jnp.max(jnp.abs(k_norm_g), axis=1))
    gain = jnp.concatenate([jnp.tile(q_norm_g[:, half], (1, N_HEADS)) * score_scale,
                            jnp.tile(k_norm_g[:, half], (1, N_KV_HEADS))], axis=1)
    w_out_a = w_out[:, head_rows, :].astype(BF16)
    w_out_c = w_out[:, aw:, :].astype(BF16)
    blk = np.kron(np.eye(2 * LANES // HEAD_DIM), np.ones((HEAD_DIM, HEAD_DIM)))
    bd = jnp.asarray(blk, BF16)
    cos, sin = _rope_tables(s)

    pad = (-(bsz + 1)) % 8
    cvec = jnp.concatenate([c, c_ctx[None, :], jnp.zeros((pad, d), F32)], axis=0)
    mods = _modulation(cvec, w_ada, b_ada)

    tm_lat = _pick(s, 1024)
    tq_lat = _pick(s, 512)
    tm_ctx = n_ctx
    tm_conv = _pick(s, 256)
    tm_ffn = _pick(s, 512)

    xc = ctx
    for l in range(depth):
        last = l == depth - 1
        m = [mods[l, :, i * d:(i + 1) * d] for i in range(6)]
        lat = [v[:bsz, None, :] for v in m]
        cx = [v[bsz:bsz + 1, None, :] for v in m]
        ng1 = norm1_g[l].reshape(1, d)
        ng2 = norm2_g[l].reshape(1, d)
        gl = gain[l].reshape(1, aw + kw)

        q, k, vt, y = _inproj(x, lat[0], lat[1], ng1, w_in_p[l], gl, bd, cos, sin, tm=tm_lat)
        qc, kc, vtc, yc = _inproj(xc, cx[0], cx[1], ng1, w_in_p[l], gl, bd, None, None,
                                  tm=tm_ctx)
        k_all = jnp.concatenate([k, kc], axis=1)
        vt_all = jnp.concatenate([vt, vtc], axis=2)
        ub = score_bound[l] * 1.01
        a_t = lax.cond(ub < ATTN_MAX_SHIFT,
                       lambda: _attention_bounded(ub.reshape(1), q, k_all, vt_all, tq=_pick(s, 2 * tq_lat),
                                                  ck=ATTN_CHUNK),
                       lambda: _attention(q, k_all, vt_all, tq=tq_lat, ck=ATTN_CHUNK))
        c_lat = _conformer_conv(y, dw_w[l], dw_b[l], conv_ln_g[l], conv_ln_b[l], tm=tm_conv)
        x = _outproj(a_t, c_lat, w_out_a[l], w_out_c[l], x, lat[2], tm=tm_ffn)
        if not last:
            a_tc = _attention(qc, kc, vtc, tq=n_ctx, ck=ATTN_CHUNK)
            c_c = _conformer_conv(yc, dw_w[l], dw_b[l], conv_ln_g[l], conv_ln_b[l], tm=n_ctx)
            xc = _outproj(a_tc, c_c, w_out_a[l], w_out_c[l], xc, cx[2], tm=n_ctx)

        i = l // 2
        if l % 2 == 0:
            w1 = ffn_w1[i:i + 1].astype(BF16)
            w3 = ffn_w3[i:i + 1].astype(BF16)
            w2 = ffn_w2[i:i + 1].astype(BF16)
            x = _ffn(x, lat[3], lat[4], lat[5], ng2, w1, w3, w2, tm=tm_ffn)
            if not last:
                xc = _ffn(xc, cx[3], cx[4], cx[5], ng2, w1, w3, w2, tm=n_ctx)
        else:
            w1 = exp_w1[i].astype(BF16)
            w3 = exp_w3[i].astype(BF16)
            w2 = exp_w2[i].astype(BF16)
            w_r = jnp.zeros((d, LANES), F32).at[:, :N_EXPERTS].set(router_w[i])
            b_r = jnp.full((1, LANES), NEG_BIG, F32).at[0, :N_EXPERTS].set(router_b[i])
            tm_l = _pick(s, MOE_ROWS)
            x = _moe(x, lat[3], lat[4], lat[5], ng2, w1, w3, w2, w_r, b_r, _tri(tm_l),
                     final_g if last else None, tm=tm_l, tile=MOE_TILE)
            if not last:
                tm_c = _pick(bsz * n_ctx, MOE_ROWS)
                xc = _moe(xc, cx[3], cx[4], cx[5], ng2, w1, w3, w2, w_r, b_r, _tri(tm_c),
                          tm=tm_c, tile=MOE_TILE)
    if depth % 2 == 0:
        return x
    return _final_norm(x, final_g, tm=tm_ffn)
```

```python
import functools

import numpy as np
import jax
import jax.numpy as jnp
from jax import lax
from jax.experimental import pallas as pl
from jax.experimental.pallas import tpu as pltpu

F32 = jnp.float32
BF16 = jnp.bfloat16

N_HEADS = 8
N_KV_HEADS = 2
HEAD_DIM = 64
GRID_W = 64
CONV_K = 31
N_EXPERTS = 8
ROPE_THETA = 10000.0
EPS = 1e-6

LANES = 128
SUBLANES = 8
MXU_COLS = 256
BF16_ROWS = 16
VMEM_LIMIT = 56 * 1024 * 1024
NEG_BIG = -1e30


def _params(*sem):
    return pltpu.CompilerParams(dimension_semantics=sem, vmem_limit_bytes=VMEM_LIMIT)


def _silu(v):
    return v / (1.0 + jnp.exp(-v))


def _mod_kernel(c_ref, w_ref, b_ref, o_ref):
    c = c_ref[...]
    o_ref[0] = jnp.dot(_silu(c), w_ref[0], preferred_element_type=F32,
                       precision=lax.Precision.HIGHEST) + b_ref[0]


def _modulation(cvec, w_ada, b_ada):
    depth, d, n = w_ada.shape
    r = cvec.shape[0]
    tn = 1536
    return pl.pallas_call(
        _mod_kernel,
        out_shape=jax.ShapeDtypeStruct((depth, r, n), F32),
        grid=(depth, n // tn),
        in_specs=[
            pl.BlockSpec((r, d), lambda l, j: (0, 0)),
            pl.BlockSpec((1, d, tn), lambda l, j: (l, 0, j)),
            pl.BlockSpec((1, 1, tn), lambda l, j: (l, 0, j)),
        ],
        out_specs=pl.BlockSpec((1, r, tn), lambda l, j: (l, 0, j)),
        compiler_params=_params("parallel", "parallel"),
        name="modulation",
    )(cvec, w_ada, b_ada.reshape(depth, 1, n))


def _norm_mod(x, ng, sc, sh):
    ms = jnp.mean(x * x, axis=-1, keepdims=True)
    return (x * lax.rsqrt(ms + EPS)) * ng * (1.0 + sc) + sh


def _inproj_kernel(*refs, use_rope):
    if use_rope:
        (x_ref, sh_ref, sc_ref, ng_ref, w_ref, gain_ref, bd_ref, cos_ref, sin_ref,
         q_ref, k_ref, vt_ref, y_ref) = refs
    else:
        (x_ref, sh_ref, sc_ref, ng_ref, w_ref, gain_ref, bd_ref,
         q_ref, k_ref, vt_ref, y_ref) = refs
    aw = N_HEADS * HEAD_DIM
    kw = N_KV_HEADS * HEAD_DIM
    qkw = aw + kw
    x = x_ref[0]
    tm = x.shape[0]
    h = _norm_mod(x, ng_ref[...], sc_ref[0], sh_ref[0]).astype(BF16)
    res = jnp.dot(h, w_ref[...], preferred_element_type=F32)

    qk = res[:, :qkw]
    sq = (qk * qk).astype(BF16)
    bd = bd_ref[...]
    parts = []
    for c0 in range(0, qkw, 2 * LANES):
        w = min(2 * LANES, qkw - c0)
        parts.append(jnp.dot(sq[:, c0:c0 + w], bd[:w, :w], preferred_element_type=F32))
    ss = jnp.concatenate(parts, axis=1)
    qk = qk * lax.rsqrt(ss * (1.0 / HEAD_DIM) + EPS) * gain_ref[...]

    if use_rope:
        cos = cos_ref[...]
        sin = sin_ref[...]
        lane = lax.broadcasted_iota(jnp.int32, (tm, LANES), 1)
        first = (lane % HEAD_DIM) < (HEAD_DIM // 2)
    outs = []
    for g in range(qkw // LANES):
        blk = qk[:, g * LANES:(g + 1) * LANES]
        if use_rope:
            half = HEAD_DIM // 2
            swapped = jnp.where(first, pltpu.roll(blk, LANES - half, 1), pltpu.roll(blk, half, 1))
            blk = blk * cos + swapped * sin
        outs.append(blk)
    q_ref[0] = jnp.concatenate(outs[:aw // LANES], axis=1).astype(BF16)
    k_ref[0] = outs[aw // LANES].astype(BF16)

    v = res[:, qkw:qkw + kw]
    vt_ref[0] = v.T.astype(BF16)

    cw = (res.shape[1] - qkw - kw) // 2
    a = res[:, qkw + kw:qkw + kw + cw]
    gt = res[:, qkw + kw + cw:]
    y_ref[0] = (a / (1.0 + jnp.exp(-gt))).astype(BF16)


def _inproj(x, sh, sc, ng, w, gain, bd, cos, sin, *, tm):
    b, n, d = x.shape
    in_w = w.shape[1]
    aw = N_HEADS * HEAD_DIM
    kw = N_KV_HEADS * HEAD_DIM
    cw = (in_w - aw - 2 * kw) // 2
    use_rope = cos is not None
    bm = sh.shape[0]
    mod_map = (lambda bi, i: (bi, 0, 0)) if bm > 1 else (lambda bi, i: (0, 0, 0))
    const2 = lambda bi, i: (0, 0)
    in_specs = [
        pl.BlockSpec((1, tm, d), lambda bi, i: (bi, i, 0)),
        pl.BlockSpec((1, 1, d), mod_map),
        pl.BlockSpec((1, 1, d), mod_map),
        pl.BlockSpec((1, d), const2),
        pl.BlockSpec((d, in_w), const2),
        pl.BlockSpec((1, aw + kw), const2),
        pl.BlockSpec((2 * LANES, 2 * LANES), const2),
    ]
    args = [x, sh, sc, ng, w, gain, bd]
    if use_rope:
        in_specs += [pl.BlockSpec((tm, LANES), lambda bi, i: (i, 0))] * 2
        args += [cos, sin]
    out_shape = (
        jax.ShapeDtypeStruct((b, n, aw), BF16),
        jax.ShapeDtypeStruct((b, n, kw), BF16),
        jax.ShapeDtypeStruct((b, kw, n), BF16),
        jax.ShapeDtypeStruct((b, n, cw), BF16),
    )
    out_specs = (
        pl.BlockSpec((1, tm, aw), lambda bi, i: (bi, i, 0)),
        pl.BlockSpec((1, tm, kw), lambda bi, i: (bi, i, 0)),
        pl.BlockSpec((1, kw, tm), lambda bi, i: (bi, 0, i)),
        pl.BlockSpec((1, tm, cw), lambda bi, i: (bi, i, 0)),
    )
    return pl.pallas_call(
        functools.partial(_inproj_kernel, use_rope=use_rope),
        out_shape=out_shape,
        grid=(b, n // tm),
        in_specs=in_specs,
        out_specs=out_specs,
        compiler_params=_params("parallel", "parallel"),
        name="inproj_lat" if use_rope else "inproj_ctx",
    )(*args)


def _attn_kernel(q_ref, k_ref, vt_ref, o_ref, m_ref, acc_ref, sa_ref, sb_ref, ma_ref, mb_ref):
    qblk = q_ref[0]
    n_keys = k_ref.shape[1]
    ck_max = sa_ref.shape[1]
    chunks = [(c0, min(ck_max, n_keys - c0)) for c0 in range(0, n_keys, ck_max)]
    lane = lax.broadcasted_iota(jnp.int32, qblk.shape, 1)
    qms = [jnp.where((lane >= hh * HEAD_DIM) & (lane < (hh + 1) * HEAD_DIM), qblk, jnp.zeros_like(qblk))
           for hh in range(N_KV_HEADS)]
    m_ref[...] = jnp.full(m_ref.shape, NEG_BIG, F32)
    acc_ref[...] = jnp.zeros(acc_ref.shape, F32)

    tq = qblk.shape[0]
    strips = [slice(j, min(j + MXU_COLS, tq)) for j in range(0, tq, MXU_COLS)]

    def scores(chunk, s_ref, mc_ref):
        c0, ck = chunk
        kc = k_ref[0, c0:c0 + ck, :]
        for hh in range(N_KV_HEADS):
            for sl in strips:
                s = lax.dot_general(kc, qms[hh][sl, :], (((1,), (1,)), ((), ())),
                                    preferred_element_type=F32)
                s_ref[hh, :ck, sl] = s
                mc_ref[hh, :, sl] = jnp.max(s, axis=0, keepdims=True)

    def consume(chunk, s_ref, mc_ref):
        c0, ck = chunk
        for hh in range(N_KV_HEADS):
            vc = vt_ref[0, hh * HEAD_DIM:(hh + 1) * HEAD_DIM, c0:c0 + ck]
            vaug = jnp.concatenate([vc, jnp.ones((BF16_ROWS, ck), BF16)], axis=0)
            for sl in strips:
                m_old = m_ref[hh, :, sl]
                m_new = jnp.maximum(m_old, mc_ref[hh, :, sl])
                alpha = jnp.exp2(m_old - m_new)
                p = jnp.exp2(s_ref[hh, :ck, sl] - m_new).astype(BF16)
                acc_ref[hh, :, sl] = (acc_ref[hh, :, sl] * alpha
                                      + jnp.dot(vaug, p, preferred_element_type=F32))
                m_ref[hh, :, sl] = m_new

    bufs = ((sa_ref, ma_ref), (sb_ref, mb_ref))
    scores(chunks[0], *bufs[0])
    for c, chunk in enumerate(chunks):
        if c + 1 < len(chunks):
            scores(chunks[c + 1], *bufs[(c + 1) % 2])
        consume(chunk, *bufs[c % 2])

    for hh in range(N_KV_HEADS):
        acc = acc_ref[hh]
        o = acc[:HEAD_DIM] / acc[HEAD_DIM:HEAD_DIM + 1]
        o_ref[0, hh * HEAD_DIM:(hh + 1) * HEAD_DIM, :] = o.astype(BF16)


def _attn_bounded_kernel(ub_ref, q_ref, k_ref, vt_ref, o_ref, *, ck_max):
    qblk = q_ref[0]
    tq = qblk.shape[0]
    n_keys = k_ref.shape[1]
    chunks = [(c0, min(ck_max, n_keys - c0)) for c0 in range(0, n_keys, ck_max)]
    strips = [slice(j, min(j + MXU_COLS, tq)) for j in range(0, tq, MXU_COLS)]
    lane = lax.broadcasted_iota(jnp.int32, qblk.shape, 1)
    qms = [jnp.where((lane >= hh * HEAD_DIM) & (lane < (hh + 1) * HEAD_DIM), qblk, jnp.zeros_like(qblk))
           for hh in range(N_KV_HEADS)]
    ub = ub_ref[0]
    items = [(chunk, hh, sl) for chunk in chunks for hh in range(N_KV_HEADS) for sl in strips]

    def probs(item):
        (c0, ck), hh, sl = item
        s = lax.dot_general(k_ref[0, c0:c0 + ck, :], qms[hh][sl, :], (((1,), (1,)), ((), ())),
                            preferred_element_type=F32)
        return jnp.exp2(s - ub).astype(BF16)

    acc = {}

    def accumulate(item, p):
        (c0, ck), hh, sl = item
        vc = vt_ref[0, hh * HEAD_DIM:(hh + 1) * HEAD_DIM, c0:c0 + ck]
        vaug = jnp.concatenate([vc, jnp.ones((BF16_ROWS, ck), BF16)], axis=0)
        part = jnp.dot(vaug, p, preferred_element_type=F32)
        key = (hh, sl.start)
        acc[key] = part if key not in acc else acc[key] + part

    p = probs(items[0])
    for i, item in enumerate(items):
        p_next = probs(items[i + 1]) if i + 1 < len(items) else None
        accumulate(item, p)
        p = p_next

    for hh in range(N_KV_HEADS):
        for sl in strips:
            a = acc[(hh, sl.start)]
            o = a[:HEAD_DIM] / a[HEAD_DIM:HEAD_DIM + 1]
            o_ref[0, hh * HEAD_DIM:(hh + 1) * HEAD_DIM, sl] = o.astype(BF16)


def _attention_bounded(ub, q, k, vt, *, tq, ck):
    b, n, aw = q.shape
    groups = aw // LANES
    _, n_keys, kw = k.shape
    return pl.pallas_call(
        functools.partial(_attn_bounded_kernel, ck_max=min(ck, n_keys)),
        out_shape=jax.ShapeDtypeStruct((b, aw, n), BF16),
        grid=(b, groups, n // tq),
        in_specs=[
            pl.BlockSpec(memory_space=pltpu.SMEM),
            pl.BlockSpec((1, tq, LANES), lambda bi, j, qi: (bi, qi, j)),
            pl.BlockSpec((1, n_keys, kw), lambda bi, j, qi: (bi, 0, 0)),
            pl.BlockSpec((1, kw, n_keys), lambda bi, j, qi: (bi, 0, 0)),
        ],
        out_specs=pl.BlockSpec((1, LANES, tq), lambda bi, j, qi: (bi, j, qi)),
        compiler_params=_params("parallel", "parallel", "parallel"),
        name="attention_bounded",
    )(ub, q, k, vt)


def _attention(q, k, vt, *, tq, ck):
    b, n, aw = q.shape
    groups = aw // LANES
    _, n_keys, kw = k.shape
    ck = min(ck, n_keys)
    return pl.pallas_call(
        _attn_kernel,
        out_shape=jax.ShapeDtypeStruct((b, aw, n), BF16),
        grid=(b, groups, n // tq),
        in_specs=[
            pl.BlockSpec((1, tq, LANES), lambda bi, j, qi: (bi, qi, j)),
            pl.BlockSpec((1, n_keys, kw), lambda bi, j, qi: (bi, 0, 0)),
            pl.BlockSpec((1, kw, n_keys), lambda bi, j, qi: (bi, 0, 0)),
        ],
        out_specs=pl.BlockSpec((1, LANES, tq), lambda bi, j, qi: (bi, j, qi)),
        scratch_shapes=[pltpu.VMEM((N_KV_HEADS, 1, tq), F32),
                        pltpu.VMEM((N_KV_HEADS, HEAD_DIM + BF16_ROWS, tq), F32),
                        pltpu.VMEM((N_KV_HEADS, ck, tq), F32),
                        pltpu.VMEM((N_KV_HEADS, ck, tq), F32),
                        pltpu.VMEM((N_KV_HEADS, 1, tq), F32),
                        pltpu.VMEM((N_KV_HEADS, 1, tq), F32)],
        compiler_params=_params("parallel", "parallel", "parallel"),
        name="attention_lat" if n_keys > n else "attention_ctx",
    )(q, k, vt)


CONV_HALO = 16
CONV_ROWS = 32


def _conv_kernel(y_ref, yp_ref, yn_ref, w_ref, b_ref, g_ref, be_ref, o_ref, buf_ref, sh_ref):
    i = pl.program_id(1)
    last = pl.num_programs(1) - 1
    tm = y_ref.shape[1]
    prev = yp_ref[0].astype(F32)
    nxt = yn_ref[0].astype(F32)
    buf_ref[0:CONV_HALO, :] = jnp.where(i > 0, prev, jnp.zeros_like(prev))
    buf_ref[CONV_HALO:CONV_HALO + tm, :] = y_ref[0].astype(F32)
    buf_ref[CONV_HALO + tm:, :] = jnp.where(i < last, nxt, jnp.zeros_like(nxt))
    span = sh_ref.shape[1]
    for res in range(1, SUBLANES):
        sh_ref[res - 1] = buf_ref[res:res + span, :]
    w = w_ref[...]
    off = CONV_HALO - CONV_K // 2
    for r in range(tm // CONV_ROWS):
        r0 = r * CONV_ROWS
        acc = jnp.zeros((CONV_ROWS, w.shape[1]), F32)
        for k in range(CONV_K):
            res = (k + off) % SUBLANES
            base = r0 + k + off - res
            if res == 0:
                tap = buf_ref[base:base + CONV_ROWS, :]
            else:
                tap = sh_ref[res - 1, base:base + CONV_ROWS, :]
            acc = acc + w[k:k + 1, :] * tap
        acc = acc + b_ref[...]
        mu = jnp.mean(acc, axis=-1, keepdims=True)
        cen = acc - mu
        var = jnp.mean(cen * cen, axis=-1, keepdims=True)
        z = cen * lax.rsqrt(var + EPS) * g_ref[...] + be_ref[...]
        o_ref[0, r0:r0 + CONV_ROWS, :] = _silu(z).astype(BF16)


def _conformer_conv(y, w_dw, b_dw, ln_g, ln_b, *, tm):
    b, n, cw = y.shape
    hb = tm // CONV_HALO
    nh = n // CONV_HALO
    vec = pl.BlockSpec((1, cw), lambda bi, i: (0, 0))
    return pl.pallas_call(
        _conv_kernel,
        out_shape=jax.ShapeDtypeStruct((b, n, cw), BF16),
        grid=(b, n // tm),
        in_specs=[
            pl.BlockSpec((1, tm, cw), lambda bi, i: (bi, i, 0)),
            pl.BlockSpec((1, CONV_HALO, cw), lambda bi, i: (bi, jnp.maximum(i * hb - 1, 0), 0)),
            pl.BlockSpec((1, CONV_HALO, cw), lambda bi, i: (bi, jnp.minimum((i + 1) * hb, nh - 1), 0)),
            pl.BlockSpec((CONV_K, cw), lambda bi, i: (0, 0)),
            vec, vec, vec,
        ],
        out_specs=pl.BlockSpec((1, tm, cw), lambda bi, i: (bi, i, 0)),
        scratch_shapes=[pltpu.VMEM((tm + 2 * CONV_HALO, cw), F32),
                        pltpu.VMEM((SUBLANES - 1, tm + 2 * CONV_HALO - SUBLANES, cw), F32)],
        compiler_params=_params("parallel", "parallel"),
        name="conformer_conv",
    )(y, y, y, w_dw, b_dw.reshape(1, cw), ln_g.reshape(1, cw), ln_b.reshape(1, cw))


def _outproj_kernel(at_ref, c_ref, wa_ref, wc_ref, x_ref, g_ref, o_ref):
    o = lax.dot_general(at_ref[0], wa_ref[...], (((0,), (0,)), ((), ())),
                        preferred_element_type=F32)
    o = o + jnp.dot(c_ref[0], wc_ref[...], preferred_element_type=F32)
    o_ref[0] = x_ref[0] + g_ref[0] * o


def _outproj(a_t, c, w_a, w_c, x, gate, *, tm):
    b, n, d = x.shape
    aw = a_t.shape[1]
    cw = c.shape[2]
    bm = gate.shape[0]
    mod_map = (lambda bi, i: (bi, 0, 0)) if bm > 1 else (lambda bi, i: (0, 0, 0))
    return pl.pallas_call(
        _outproj_kernel,
        out_shape=jax.ShapeDtypeStruct((b, n, d), F32),
        grid=(b, n // tm),
        in_specs=[
            pl.BlockSpec((1, aw, tm), lambda bi, i: (bi, 0, i)),
            pl.BlockSpec((1, tm, cw), lambda bi, i: (bi, i, 0)),
            pl.BlockSpec((aw, d), lambda bi, i: (0, 0)),
            pl.BlockSpec((cw, d), lambda bi, i: (0, 0)),
            pl.BlockSpec((1, tm, d), lambda bi, i: (bi, i, 0)),
            pl.BlockSpec((1, 1, d), mod_map),
        ],
        out_specs=pl.BlockSpec((1, tm, d), lambda bi, i: (bi, i, 0)),
        compiler_params=_params("parallel", "parallel"),
        name="outproj",
    )(a_t, c, w_a, w_c, x, gate)


def _swiglu(h, w1_ref, w3_ref, w2_ref, ff_chunk):
    d_ff = w1_ref.shape[2]
    f = None
    for c0 in range(0, d_ff, ff_chunk):
        a = jnp.dot(h, w1_ref[0, :, c0:c0 + ff_chunk], preferred_element_type=F32)
        b = jnp.dot(h, w3_ref[0, :, c0:c0 + ff_chunk], preferred_element_type=F32)
        z = (_silu(a) * b).astype(BF16)
        part = jnp.dot(z, w2_ref[0, c0:c0 + ff_chunk, :], preferred_element_type=F32)
        f = part if f is None else f + part
    return f


def _ffn_kernel(x_ref, sh_ref, sc_ref, g_ref, ng_ref, w1_ref, w3_ref, w2_ref, o_ref, *, ff_chunk):
    h = _norm_mod(x_ref[0], ng_ref[...], sc_ref[0], sh_ref[0]).astype(BF16)
    o_ref[0] = x_ref[0] + g_ref[0] * _swiglu(h, w1_ref, w3_ref, w2_ref, ff_chunk)


def _ffn(x, sh, sc, gate, ng, w1, w3, w2, *, tm):
    b, n, d = x.shape
    d_ff = w1.shape[2]
    bm = sh.shape[0]
    mod_map = (lambda bi, i: (bi, 0, 0)) if bm > 1 else (lambda bi, i: (0, 0, 0))
    mod = pl.BlockSpec((1, 1, d), mod_map)
    return pl.pallas_call(
        functools.partial(_ffn_kernel, ff_chunk=d_ff // 2),
        out_shape=jax.ShapeDtypeStruct((b, n, d), F32),
        grid=(b, n // tm),
        in_specs=[
            pl.BlockSpec((1, tm, d), lambda bi, i: (bi, i, 0)),
            mod, mod, mod,
            pl.BlockSpec((1, d), lambda bi, i: (0, 0)),
            pl.BlockSpec((1, d, d_ff), lambda bi, i: (0, 0, 0)),
            pl.BlockSpec((1, d, d_ff), lambda bi, i: (0, 0, 0)),
            pl.BlockSpec((1, d_ff, d), lambda bi, i: (0, 0, 0)),
        ],
        out_specs=pl.BlockSpec((1, tm, d), lambda bi, i: (bi, i, 0)),
        compiler_params=_params("parallel", "parallel"),
        name="dense_ffn",
    )(x, sh, sc, gate, ng, w1, w3, w2)


META_G1, META_G2, META_E1, META_E2, META_R1, META_R2 = range(6)
TOK_ROWS = 8


def _to_token_tiles(ref, val, lead=()):
    n = val.shape[0]
    for c in range(TOK_ROWS):
        ref[lead + (pl.ds(c, n, stride=TOK_ROWS), slice(None))] = val[:, c * LANES:(c + 1) * LANES]


def _from_token_tiles(ref, n, lead=()):
    return jnp.concatenate([ref[lead + (pl.ds(c, n, stride=TOK_ROWS), slice(None))]
                            for c in range(TOK_ROWS)], axis=1)


def _router_kernel(x_ref, sh_ref, sc_ref, ng_ref, wr_ref, br_ref, tri_ref,
                   h_ref, meta_ref, cnt_ref, carry_ref):
    i = pl.program_id(0)

    @pl.when(i == 0)
    def _():
        carry_ref[...] = jnp.zeros(carry_ref.shape, F32)

    h = _norm_mod(x_ref[...], ng_ref[...], sc_ref[0], sh_ref[0])
    _to_token_tiles(h_ref, h)
    w = wr_ref[...]
    w_hi = w.astype(BF16)
    w_lo = (w - w_hi.astype(F32)).astype(BF16)
    h_hi = h.astype(BF16)
    h_lo = (h - h_hi.astype(F32)).astype(BF16)
    logits = (jnp.dot(h_hi, w_hi, preferred_element_type=F32)
              + (jnp.dot(h_lo, w_hi, preferred_element_type=F32)
                 + jnp.dot(h_hi, w_lo, preferred_element_type=F32))) + br_ref[...]
    lane = lax.broadcasted_iota(jnp.int32, logits.shape, 1)
    t1 = jnp.max(logits, axis=-1, keepdims=True)
    i1 = jnp.min(jnp.where(logits == t1, lane, LANES), axis=-1, keepdims=True)
    rest = jnp.where(lane == i1, NEG_BIG, logits)
    t2 = jnp.max(rest, axis=-1, keepdims=True)
    i2 = jnp.min(jnp.where(rest == t2, lane, LANES), axis=-1, keepdims=True)
    e2 = jnp.exp(t2 - t1)
    den = 1.0 + e2
    onehot = jnp.where((lane == i1) | (lane == i2), 1.0, 0.0)
    before = carry_ref[...] + jnp.dot(tri_ref[...], onehot.astype(BF16), preferred_element_type=F32)
    r1 = jnp.sum(jnp.where(lane == i1, before, 0.0), axis=-1, keepdims=True)
    r2 = jnp.sum(jnp.where(lane == i2, before, 0.0), axis=-1, keepdims=True)
    meta = jnp.zeros(logits.shape, F32)
    for col, val in ((META_G1, 1.0 / den), (META_G2, e2 / den), (META_E1, i1.astype(F32)),
                     (META_E2, i2.astype(F32)), (META_R1, r1), (META_R2, r2)):
        meta = jnp.where(lane == col, val, meta)
    meta_ref[...] = meta
    carry_ref[...] = carry_ref[...] + jnp.sum(onehot, axis=0, keepdims=True)
    cnt_ref[...] = carry_ref[...]


def _router(x2, sh, sc, ng, w_r, b_r, tri, *, tm, rows_per_mod):
    n, d = x2.shape
    per = rows_per_mod // tm
    mod_map = lambda i: (i // per, 0, 0)
    const = lambda i: (0, 0)
    return pl.pallas_call(
        _router_kernel,
        out_shape=(jax.ShapeDtypeStruct((n * TOK_ROWS, LANES), F32),
                   jax.ShapeDtypeStruct((n, LANES), F32),
                   jax.ShapeDtypeStruct((1, LANES), F32)),
        grid=(n // tm,),
        in_specs=[
            pl.BlockSpec((tm, d), lambda i: (i, 0)),
            pl.BlockSpec((1, 1, d), mod_map),
            pl.BlockSpec((1, 1, d), mod_map),
            pl.BlockSpec((1, d), const),
            pl.BlockSpec((d, LANES), const),
            pl.BlockSpec((1, LANES), const),
            pl.BlockSpec((tm, tm), const),
        ],
        out_specs=(pl.BlockSpec((tm * TOK_ROWS, LANES), lambda i: (i, 0)),
                   pl.BlockSpec((tm, LANES), lambda i: (i, 0)),
                   pl.BlockSpec((1, LANES), const)),
        scratch_shapes=[pltpu.VMEM((1, LANES), F32)],
        compiler_params=_params("arbitrary"),
        name="moe_router",
    )(x2, sh, sc, ng, w_r, b_r, tri)


def _tok(ref, t, count=1):
    start = t * TOK_ROWS
    if not isinstance(t, int):
        start = pl.multiple_of(start, TOK_ROWS)
    return ref.at[pl.ds(start, count * TOK_ROWS)]


def _row_copy(src_ref, src_row, dst_ref, dst_row, sem):
    return pltpu.make_async_copy(_tok(src_ref, src_row), _tok(dst_ref, dst_row), sem)


ROW_UNROLL = 8


def _scatter_kernel(tail_ref, pos_ref, h_ref, hs_ref, zero_ref, sem):
    i = pl.program_id(0)
    tm = h_ref.shape[0] // TOK_ROWS
    tile = zero_ref.shape[0] // TOK_ROWS

    @pl.when(i == 0)
    def _():
        zero_ref[...] = jnp.zeros(zero_ref.shape, F32)
        n_tiles = hs_ref.shape[0] // zero_ref.shape[0]
        n_used = tail_ref[N_EXPERTS]

        def zero_tile(first_row):
            return pltpu.make_async_copy(zero_ref, _tok(hs_ref, first_row, tile), sem)

        for e in range(N_EXPERTS):
            zero_tile(tail_ref[e]).start()
        lax.fori_loop(n_used, n_tiles, lambda t, c: (zero_tile(t * tile).start(), c)[1], 0)
        for e in range(N_EXPERTS):
            zero_tile(tail_ref[e]).wait()
        lax.fori_loop(n_used, n_tiles, lambda t, c: (zero_tile(t * tile).wait(), c)[1], 0)

    def issue(r, carry):
        for k in range(2):
            _row_copy(h_ref, r, hs_ref, pos_ref[0, 0, k * tm + r], sem).start(priority=k)
        return carry

    lax.fori_loop(0, tm, issue, 0, unroll=ROW_UNROLL)

    def drain(r, carry):
        _row_copy(h_ref, 0, hs_ref, 0, sem).wait()
        return carry

    lax.fori_loop(0, 2 * tm, drain, 0, unroll=ROW_UNROLL)


def _scatter(tails, pos, h, *, tm, n_rows, tile):
    n = h.shape[0] // TOK_ROWS
    return pl.pallas_call(
        _scatter_kernel,
        out_shape=jax.ShapeDtypeStruct((n_rows * TOK_ROWS, LANES), F32),
        grid_spec=pltpu.PrefetchScalarGridSpec(
            num_scalar_prefetch=1,
            grid=(n // tm,),
            in_specs=[
                pl.BlockSpec((1, 1, 2 * tm), lambda i, tails: (i, 0, 0), memory_space=pltpu.SMEM),
                pl.BlockSpec((tm * TOK_ROWS, LANES), lambda i, tails: (i, 0)),
            ],
            out_specs=pl.BlockSpec(memory_space=pl.ANY),
            scratch_shapes=[pltpu.VMEM((tile * TOK_ROWS, LANES), F32), pltpu.SemaphoreType.DMA],
        ),
        compiler_params=_params("arbitrary"),
        name="moe_scatter",
    )(tails, pos, h)


def _expert_kernel(te_ref, nu_ref, hs_ref, w1_ref, w3_ref, w2_ref, y_ref, *, ff_chunk):
    used = pl.program_id(0) < nu_ref[0]

    @pl.when(used)
    def _():
        tile = hs_ref.shape[0] // TOK_ROWS
        h = _from_token_tiles(hs_ref, tile).astype(BF16)
        _to_token_tiles(y_ref, _swiglu(h, w1_ref, w3_ref, w2_ref, ff_chunk))

    @pl.when(jnp.logical_not(used))
    def _():
        y_ref[...] = jnp.zeros(y_ref.shape, F32)


def _experts(tile_e, n_used, hs, w1, w3, w2, *, tile):
    n_rows = hs.shape[0] // TOK_ROWS
    d = w1.shape[1]
    d_ff = w1.shape[2]
    row_map = lambda t, te, nu: (t, 0)
    w_map = lambda t, te, nu: (te[t], 0, 0)
    return pl.pallas_call(
        functools.partial(_expert_kernel, ff_chunk=d_ff // 2),
        out_shape=jax.ShapeDtypeStruct((n_rows * TOK_ROWS, LANES), F32),
        grid_spec=pltpu.PrefetchScalarGridSpec(
            num_scalar_prefetch=2,
            grid=(n_rows // tile,),
            in_specs=[
                pl.BlockSpec((tile * TOK_ROWS, LANES), row_map),
                pl.BlockSpec((1, d, d_ff), w_map),
                pl.BlockSpec((1, d, d_ff), w_map),
                pl.BlockSpec((1, d_ff, d), w_map),
            ],
            out_specs=pl.BlockSpec((tile * TOK_ROWS, LANES), row_map),
        ),
        compiler_params=_params("arbitrary"),
        name="moe_experts",
    )(tile_e, n_used, hs, w1, w3, w2)


def _combine_kernel(pos_ref, pos_next_ref, x_ref, g_ref, meta_ref, y_ref, *rest, final_norm):
    fg_ref = rest[0] if final_norm else None
    o_ref, buf_ref, sems = rest[-3:]
    tm = x_ref.shape[0]
    i = pl.program_id(0)
    slot = i % 2

    def gather(p_ref, dst):
        def issue(r, carry):
            for k in range(2):
                _row_copy(y_ref, p_ref[0, 0, k * tm + r], buf_ref.at[dst, k], r,
                          sems.at[dst]).start(priority=k)
            return carry
        lax.fori_loop(0, tm, issue, 0, unroll=ROW_UNROLL)

    @pl.when(i == 0)
    def _():
        gather(pos_ref, 0)

    @pl.when(i + 1 < pl.num_programs(0))
    def _():
        gather(pos_next_ref, 1 - slot)

    def drain(r, carry):
        _row_copy(y_ref, 0, buf_ref.at[slot, 0], 0, sems.at[slot]).wait()
        return carry

    lax.fori_loop(0, 2 * tm, drain, 0, unroll=ROW_UNROLL)
    meta = meta_ref[...]
    g1 = meta[:, META_G1:META_G1 + 1]
    g2 = meta[:, META_G2:META_G2 + 1]
    y1 = _from_token_tiles(buf_ref, tm, (slot, 0))
    y2 = _from_token_tiles(buf_ref, tm, (slot, 1))
    out = x_ref[...] + g_ref[0] * (g1 * y1 + g2 * y2)
    if final_norm:
        ms = jnp.mean(out * out, axis=-1, keepdims=True)
        out = out * lax.rsqrt(ms + EPS) * fg_ref[...]
    o_ref[...] = out


def _combine(pos, x2, gate, meta, y, final_g=None, *, tm, rows_per_mod):
    n, d = x2.shape
    per = rows_per_mod // tm
    last = n // tm - 1
    in_specs = [
        pl.BlockSpec((1, 1, 2 * tm), lambda i: (i, 0, 0), memory_space=pltpu.SMEM),
        pl.BlockSpec((1, 1, 2 * tm), lambda i: (jnp.minimum(i + 1, last), 0, 0), memory_space=pltpu.SMEM),
        pl.BlockSpec((tm, d), lambda i: (i, 0)),
        pl.BlockSpec((1, 1, d), lambda i: (i // per, 0, 0)),
        pl.BlockSpec((tm, LANES), lambda i: (i, 0)),
        pl.BlockSpec(memory_space=pl.ANY),
    ]
    args = [pos, pos, x2, gate, meta, y]
    if final_g is not None:
        in_specs.append(pl.BlockSpec((1, d), lambda i: (0, 0)))
        args.append(final_g.reshape(1, d))
    return pl.pallas_call(
        functools.partial(_combine_kernel, final_norm=final_g is not None),
        out_shape=jax.ShapeDtypeStruct((n, d), F32),
        grid=(n // tm,),
        in_specs=in_specs,
        out_specs=pl.BlockSpec((tm, d), lambda i: (i, 0)),
        scratch_shapes=[pltpu.VMEM((2, 2, tm * TOK_ROWS, LANES), F32), pltpu.SemaphoreType.DMA((2,))],
        compiler_params=_params("arbitrary"),
        name="moe_combine",
    )(*args)


def _moe(x, sh, sc, gate, ng, w1, w3, w2, w_r, b_r, tri, final_g=None, *, tm, tile):
    b, n, d = x.shape
    rows = b * n
    rows_per_mod = n if sh.shape[0] > 1 else rows
    x2 = x.reshape(rows, d)
    h, meta, cnt = _router(x2, sh, sc, ng, w_r, b_r, tri, tm=tm, rows_per_mod=rows_per_mod)

    counts = cnt[0, :N_EXPERTS].astype(jnp.int32)
    padded = (counts + tile - 1) // tile * tile
    ends = jnp.cumsum(padded)
    starts = ends - padded
    experts = meta[:, META_E1:META_E2 + 1].astype(jnp.int32)
    ranks = meta[:, META_R1:META_R2 + 1].astype(jnp.int32)
    pos = starts[experts] + ranks
    pos = pos.reshape(rows // tm, tm, 2).transpose(0, 2, 1).reshape(rows // tm, 1, 2 * tm)
    n_rows = 2 * rows + N_EXPERTS * tile
    n_tiles = n_rows // tile
    n_used = (ends[-1] // tile).astype(jnp.int32)
    t = jnp.minimum(jnp.arange(n_tiles, dtype=jnp.int32), n_used - 1)
    tile_e = jnp.minimum(jnp.searchsorted(ends, t * tile, side="right"), N_EXPERTS - 1).astype(jnp.int32)
    busiest = jnp.argmax(counts)
    tails = jnp.where(padded > 0, ends - tile, ends[busiest] - tile).astype(jnp.int32)
    tails = jnp.concatenate([tails, n_used.reshape(1)])

    hs = _scatter(tails, pos, h, tm=tm, n_rows=n_rows, tile=tile)
    y = _experts(tile_e, n_used.reshape(1), hs, w1, w3, w2, tile=tile)
    out = _combine(pos, x2, gate, meta, y, final_g, tm=tm, rows_per_mod=rows_per_mod)
    return out.reshape(b, n, d)


def _final_kernel(x_ref, g_ref, o_ref):
    x = x_ref[0]
    ms = jnp.mean(x * x, axis=-1, keepdims=True)
    o_ref[0] = x * lax.rsqrt(ms + EPS) * g_ref[...]


def _final_norm(x, g, *, tm):
    b, n, d = x.shape
    return pl.pallas_call(
        _final_kernel,
        out_shape=jax.ShapeDtypeStruct((b, n, d), F32),
        grid=(b, n // tm),
        in_specs=[pl.BlockSpec((1, tm, d), lambda bi, i: (bi, i, 0)),
                  pl.BlockSpec((1, d), lambda bi, i: (0, 0))],
        out_specs=pl.BlockSpec((1, tm, d), lambda bi, i: (bi, i, 0)),
        compiler_params=_params("parallel", "parallel"),
        name="final_norm",
    )(x, g.reshape(1, d))


def _qk_column_order():
    half = np.concatenate([np.arange(0, HEAD_DIM, 2), np.arange(1, HEAD_DIM, 2)])
    per_group = N_HEADS // N_KV_HEADS
    heads = [h for j in range(per_group) for h in range(j, N_HEADS, per_group)]
    q_cols = np.concatenate([h * HEAD_DIM + half for h in heads])
    k_cols = N_HEADS * HEAD_DIM + np.concatenate([g * HEAD_DIM + half for g in range(N_KV_HEADS)])
    head_rows = np.concatenate([h * HEAD_DIM + np.arange(HEAD_DIM) for h in heads])
    return half, q_cols, k_cols, head_rows


def _rope_tables(n_tokens):
    rows = n_tokens // GRID_W
    row = jnp.repeat(jnp.arange(rows), GRID_W).astype(F32)
    col = jnp.tile(jnp.arange(GRID_W), rows).astype(F32)
    half = HEAD_DIM // 2
    inv = ROPE_THETA ** (-jnp.arange(0, half, 2, dtype=F32) / half)
    ang = jnp.concatenate([row[:, None] * inv, col[:, None] * inv], axis=-1)
    cos, sin = jnp.cos(ang), jnp.sin(ang)
    reps = LANES // HEAD_DIM
    return (jnp.tile(jnp.concatenate([cos, cos], axis=-1), (1, reps)),
            jnp.tile(jnp.concatenate([-sin, sin], axis=-1), (1, reps)))


ATTN_CHUNK = 512
ATTN_MAX_SHIFT = 48.0
MOE_ROWS = 512
MOE_TILE = 512


def _tri(n):
    return jnp.asarray(np.tril(np.ones((n, n), np.float32), -1), BF16)


def _pick(n, pref):
    t = min(n, pref)
    while n % t:
        t //= 2
    return t


def kernel(x, c, ctx, c_ctx, w_ada, b_ada, norm1_g, w_in, q_norm_g, k_norm_g, dw_w, dw_b,
           conv_ln_g, conv_ln_b, w_out, norm2_g, ffn_w1, ffn_w3, ffn_w2, router_w, router_b,
           exp_w1, exp_w3, exp_w2, final_g):
    bsz, s, d = x.shape
    n_ctx = ctx.shape[1]
    depth = w_ada.shape[0]
    aw = N_HEADS * HEAD_DIM
    kw = N_KV_HEADS * HEAD_DIM

    half, q_cols, k_cols, head_rows = _qk_column_order()
    cols = np.concatenate([q_cols, k_cols, np.arange(aw + kw, w_in.shape[2])])
    w_in_p = w_in[:, :, cols].astype(BF16)
    score_scale = HEAD_DIM ** -0.5 * np.log2(np.e)
    score_bound = (HEAD_DIM * score_scale * jnp.max(jnp.abs(q_norm_g), axis=1)
                   * jnp.max(jnp.abs(k_norm_g), axis=1))
    gain = jnp.concatenate([jnp.tile(q_norm_g[:, half], (1, N_HEADS)) * score_scale,
                            jnp.tile(k_norm_g[:, half], (1, N_KV_HEADS))], axis=1)
    w_out_a = w_out[:, head_rows, :].astype(BF16)
    w_out_c = w_out[:, aw:, :].astype(BF16)
    blk = np.kron(np.eye(2 * LANES // HEAD_DIM), np.ones((HEAD_DIM, HEAD_DIM)))
    bd = jnp.asarray(blk, BF16)
    cos, sin = _rope_tables(s)

    pad = (-(bsz + 1)) % 8
    cvec = jnp.concatenate([c, c_ctx[None, :], jnp.zeros((pad, d), F32)], axis=0)
    mods = _modulation(cvec, w_ada, b_ada)

    tm_lat = _pick(s, 1024)
    tq_lat = _pick(s, 512)
    tm_ctx = n_ctx
    tm_conv = _pick(s, 256)
    tm_ffn = _pick(s, 512)

    xc = ctx
    for l in range(depth):
        last = l == depth - 1
        m = [mods[l, :, i * d:(i + 1) * d] for i in range(6)]
        lat = [v[:bsz, None, :] for v in m]
        cx = [v[bsz:bsz + 1, None, :] for v in m]
        ng1 = norm1_g[l].reshape(1, d)
        ng2 = norm2_g[l].reshape(1, d)
        gl = gain[l].reshape(1, aw + kw)

        q, k, vt, y = _inproj(x, lat[0], lat[1], ng1, w_in_p[l], gl, bd, cos, sin, tm=tm_lat)
        qc, kc, vtc, yc = _inproj(xc, cx[0], cx[1], ng1, w_in_p[l], gl, bd, None, None,
                                  tm=tm_ctx)
        k_all = jnp.concatenate([k, kc], axis=1)
        vt_all = jnp.concatenate([vt, vtc], axis=2)
        ub = score_bound[l] * 1.01
        a_t = lax.cond(ub < ATTN_MAX_SHIFT,
                       lambda: _attention_bounded(ub.reshape(1), q, k_all, vt_all, tq=_pick(s, 4 * tq_lat),
                                                  ck=ATTN_CHUNK),
                       lambda: _attention(q, k_all, vt_all, tq=tq_lat, ck=ATTN_CHUNK))
        c_lat = _conformer_conv(y, dw_w[l], dw_b[l], conv_ln_g[l], conv_ln_b[l], tm=tm_conv)
        x = _outproj(a_t, c_lat, w_out_a[l], w_out_c[l], x, lat[2], tm=tm_ffn)
        if not last:
            a_tc = _attention(qc, kc, vtc, tq=n_ctx, ck=ATTN_CHUNK)
            c_c = _conformer_conv(yc, dw_w[l], dw_b[l], conv_ln_g[l], conv_ln_b[l], tm=n_ctx)
            xc = _outproj(a_tc, c_c, w_out_a[l], w_out_c[l], xc, cx[2], tm=n_ctx)

        i = l // 2
        if l % 2 == 0:
            w1 = ffn_w1[i:i + 1].astype(BF16)
            w3 = ffn_w3[i:i + 1].astype(BF16)
            w2 = ffn_w2[i:i + 1].astype(BF16)
            x = _ffn(x, lat[3], lat[4], lat[5], ng2, w1, w3, w2, tm=tm_ffn)
            if not last:
                xc = _ffn(xc, cx[3], cx[4], cx[5], ng2, w1, w3, w2, tm=n_ctx)
        else:
            w1 = exp_w1[i].astype(BF16)
            w3 = exp_w3[i].astype(BF16)
            w2 = exp_w2[i].astype(BF16)
            w_r = jnp.zeros((d, LANES), F32).at[:, :N_EXPERTS].set(router_w[i])
            b_r = jnp.full((1, LANES), NEG_BIG, F32).at[0, :N_EXPERTS].set(router_b[i])
            tm_l = _pick(s, MOE_ROWS)
            x = _moe(x, lat[3], lat[4], lat[5], ng2, w1, w3, w2, w_r, b_r, _tri(tm_l),
                     final_g if last else None, tm=tm_l, tile=MOE_TILE)
            if not last:
                tm_c = _pick(bsz * n_ctx, MOE_ROWS)
                xc = _moe(xc, cx[3], cx[4], cx[5], ng2, w1, w3, w2, w_r, b_r, _tri(tm_c),
                          tm=tm_c, tile=MOE_TILE)
    if depth % 2 == 0:
        return x
    return _final_norm(x, final_g, tm=tm_ffn)
```
